```python
import jax
import jax.numpy as jnp
from jax import lax
import numpy as np

D_MODEL = 4096
BATCH = 4
SEQ = 2048
DEPTH = 1
DEC_BATCH = 128
DEC_SEQ = 4
PAST_LEN = 16384
PAGE_SIZE = 128

MIX_WIDTH = D_MODEL
ATTN_WIDTH = MIX_WIDTH // 2
CONV_CH = MIX_WIDTH - ATTN_WIDTH
V_HEAD = 128
N_HEADS = ATTN_WIDTH // V_HEAD
QK_NOPE = 128
QK_ROPE = 64
QK_HEAD = QK_NOPE + QK_ROPE
Q_LORA = 1024
KV_LORA = 512
ROPE_THETA = 10000.0
SOFTMAX_SCALE = QK_HEAD ** -0.5
Q_BLOCK = 128
CONV_WIDTH = 31
N_EXPERT_GROUPS = 4
EXPERTS_PER_GROUP = 8
N_EXPERTS = N_EXPERT_GROUPS * EXPERTS_PER_GROUP
TOP_K = 2
D_EXPERT = 1024
NORM_EPS = 1e-6
LN_EPS = 1e-5
SPLIT_POINTS = (Q_LORA, Q_LORA + KV_LORA, Q_LORA + KV_LORA + QK_ROPE,
                Q_LORA + KV_LORA + QK_ROPE + CONV_CH)
IN_COLS = SPLIT_POINTS[3] + CONV_CH

kernel_name = 'hymba_mla_conformer_hmoe_step'


def _rms_norm(x, g):
    xf = x.astype(jnp.float32)
    y = xf * lax.rsqrt(jnp.mean(xf * xf, axis=-1, keepdims=True) + NORM_EPS)
    return (y * g.astype(jnp.float32)).astype(x.dtype)


def _layer_norm(x, g, b):
    xf = x.astype(jnp.float32)
    mu = jnp.mean(xf, axis=-1, keepdims=True)
    var = jnp.mean(jnp.square(xf - mu), axis=-1, keepdims=True)
    y = (xf - mu) * lax.rsqrt(var + LN_EPS)
    return (y * g.astype(jnp.float32) + b.astype(jnp.float32)).astype(x.dtype)


def _rope(x, pos, per_head):
    half = QK_ROPE // 2
    freq = ROPE_THETA ** (-jnp.arange(half, dtype=jnp.float32) / half)
    ang = pos.astype(jnp.float32)[:, None] * freq[None, :]
    if per_head:
        ang = ang[:, None, :]
    cos, sin = jnp.cos(ang), jnp.sin(ang)
    xf = x.astype(jnp.float32)
    x1, x2 = xf[..., :half], xf[..., half:]
    return jnp.concatenate([x1 * cos - x2 * sin, x1 * sin + x2 * cos], axis=-1).astype(x.dtype)


def _mla_attend(q_lat, q_rope, kv_lat, k_rope, q_pos, k_pos):
    s = (jnp.einsum('nqhc,nkc->nhqk', q_lat, kv_lat).astype(jnp.float32)
         + jnp.einsum('nqhr,nkr->nhqk', q_rope, k_rope).astype(jnp.float32)) * SOFTMAX_SCALE
    s = jnp.where(k_pos[None, :] <= q_pos[:, None], s, -jnp.inf)
    p = jax.nn.softmax(s, axis=-1).astype(kv_lat.dtype)
    return jnp.einsum('nhqk,nkc->nqhc', p, kv_lat)


def _attend_prompt(q_lat, q_rope, kv_lat, k_rope):
    n, t, h, c = q_lat.shape
    nb = t // Q_BLOCK
    k_pos = jnp.arange(t)
    qb = q_lat.reshape(n, nb, Q_BLOCK, h, c).swapaxes(0, 1)
    rb = q_rope.reshape(n, nb, Q_BLOCK, h, QK_ROPE).swapaxes(0, 1)
    pb = k_pos.reshape(nb, Q_BLOCK)
    out = lax.map(lambda a: _mla_attend(a[0], a[1], kv_lat, k_rope, a[2], k_pos), (qb, rb, pb))
    return out.swapaxes(0, 1).reshape(n, t, h, c)


def _make_sample_attend(cache_lat, cache_rope, page_table):
    def attend(q_lat, q_rope, kv_lat, k_rope):
        t = q_lat.shape[1]
        past = page_table.shape[1] * PAGE_SIZE
        k_pos = jnp.arange(past + t)
        q_pos = past + jnp.arange(t)

        def one_seq(a):
            ql, qr, kl_new, kr_new, pt = a
            kl = jnp.concatenate([cache_lat[pt].reshape(past, KV_LORA), kl_new.astype(cache_lat.dtype)], axis=0)
            kr = jnp.concatenate([cache_rope[pt].reshape(past, QK_ROPE), kr_new.astype(cache_rope.dtype)], axis=0)
            return _mla_attend(ql[None], qr[None], kl[None], kr[None], q_pos, k_pos)[0]

        return lax.map(one_seq, (q_lat, q_rope, kv_lat, k_rope, page_table))
    return attend


def _causal_depthwise_conv(u_ext, w, b):
    y = lax.conv_general_dilated(u_ext, w[:, None, :].astype(u_ext.dtype), window_strides=(1,),
                                 padding='VALID', dimension_numbers=('NWC', 'WIO', 'NWC'),
                                 feature_group_count=u_ext.shape[-1])
    return y + b.astype(y.dtype)


def _hier_moe(h, w_rg, b_rg, w_re, b_re, w_eg, w_eu, w_ed):
    g_prob = jax.nn.softmax((h @ w_rg).astype(jnp.float32) + b_rg.astype(jnp.float32), axis=-1)
    g_w, g_idx = lax.top_k(g_prob, 1)
    e_logits = ((h @ w_re).astype(jnp.float32) + b_re.astype(jnp.float32)).reshape(
        -1, N_EXPERT_GROUPS, EXPERTS_PER_GROUP)
    e_in = jnp.take_along_axis(e_logits, g_idx[:, :, None], axis=1)[:, 0]
    top_w, top_i = lax.top_k(jax.nn.softmax(e_in, axis=-1), TOP_K)
    top_w = top_w / jnp.sum(top_w, axis=-1, keepdims=True) * g_w
    eid = g_idx * EXPERTS_PER_GROUP + top_i
    combine = jnp.sum(jax.nn.one_hot(eid, N_EXPERTS, dtype=jnp.float32) * top_w[..., None], axis=1)
    out = jnp.zeros(h.shape, jnp.float32)
    for e in range(N_EXPERTS):
        y = (jax.nn.silu(h @ w_eg[e]) * (h @ w_eu[e])) @ w_ed[e]
        out = out + combine[:, e:e + 1] * y.astype(jnp.float32)
    return out.astype(h.dtype)


def _decoder_layer(x, c, pos, conv_hist, attend, w_ada, b_ada, g_pre_mix, g_post_mix, g_pre_ffn,
                   g_post_ffn, w_in, g_q_lat, g_kv_lat, w_uq, w_uk, w_uv, conv_w, conv_b, conv_ln_g,
                   conv_ln_b, w_out, w_rg, b_rg, w_re, b_re, w_eg, w_eu, w_ed):
    n, t, _ = x.shape
    mod = (jax.nn.silu(c) @ w_ada + b_ada)[:, None, :]
    sh1, sc1, ga1, sh2, sc2, ga2 = jnp.split(mod, 6, axis=-1)
    h = _rms_norm(x, g_pre_mix) * (1 + sc1) + sh1
    z = h @ w_in
    q_a, kv_a, k_r, u_val, u_gate = jnp.split(z, list(SPLIT_POINTS), axis=-1)
    q = jnp.einsum('ntl,lhd->nthd', _rms_norm(q_a, g_q_lat), w_uq)
    q_rope = _rope(q[..., QK_NOPE:], pos, True)
    q_lat = jnp.einsum('nthd,chd->nthc', q[..., :QK_NOPE], w_uk)
    kv_lat = _rms_norm(kv_a, g_kv_lat)
    k_rope = _rope(k_r, pos, False)
    o_lat = attend(q_lat, q_rope, kv_lat, k_rope)
    attn = jnp.einsum('nthc,chv->nthv', o_lat, w_uv).reshape(n, t, ATTN_WIDTH)
    u = u_val * jax.nn.sigmoid(u_gate)
    u_ext = jnp.concatenate([conv_hist.astype(u.dtype), u], axis=1)
    conv = jax.nn.silu(_layer_norm(_causal_depthwise_conv(u_ext, conv_w, conv_b), conv_ln_g, conv_ln_b))
    mix = jnp.concatenate([attn, conv], axis=-1) @ w_out
    x = x + ga1 * _rms_norm(mix, g_post_mix)
    h2 = _rms_norm(x, g_pre_ffn) * (1 + sc2) + sh2
    f = _hier_moe(h2.reshape(n * t, -1), w_rg, b_rg, w_re, b_re, w_eg, w_eu, w_ed).reshape(n, t, -1)
    x = x + ga2 * _rms_norm(f, g_post_ffn)
    return x, kv_lat, k_rope, u_ext[:, -(CONV_WIDTH - 1):]


def setup_inputs(seed: int = 0) -> dict:
    key = jax.random.key(seed)
    k = jax.random.split(key, 40)
    n_pages = PAST_LEN // PAGE_SIZE
    n_pool = (DEC_BATCH * n_pages * 5) // 4

    def nrm(i, shape, scale):
        return scale * jax.random.normal(k[i], shape, jnp.float32)

    def gain(i, shape):
        return 1.0 + nrm(i, shape, 0.02)

    page_table = jax.random.permutation(k[7], n_pool)[:DEC_BATCH * n_pages].reshape(
        DEC_BATCH, n_pages).astype(jnp.int32)
    return {
        'x_prompt': nrm(0, (BATCH, SEQ, D_MODEL), 1.0),
        'x_sample': nrm(1, (DEC_BATCH, DEC_SEQ, D_MODEL), 1.0),
        'c_prompt': nrm(2, (BATCH, D_MODEL), 1.0),
        'c_sample': nrm(3, (DEC_BATCH, D_MODEL), 1.0),
        'cache_kv_latent': nrm(4, (DEPTH, n_pool, PAGE_SIZE, KV_LORA), 1.0),
        'cache_k_rope': nrm(5, (DEPTH, n_pool, PAGE_SIZE, QK_ROPE), 1.0),
        'state_conv': nrm(6, (DEPTH, DEC_BATCH, CONV_WIDTH - 1, CONV_CH), 0.5),
        'page_table': page_table,
        'w_ada': nrm(8, (DEPTH, D_MODEL, 6 * D_MODEL), 0.5 * D_MODEL ** -0.5),
        'b_ada': nrm(9, (DEPTH, 6 * D_MODEL), 0.01),
        'g_pre_mix': gain(10, (DEPTH, D_MODEL)),
        'g_post_mix': gain(11, (DEPTH, D_MODEL)),
        'g_pre_ffn': gain(12, (DEPTH, D_MODEL)),
        'g_post_ffn': gain(13, (DEPTH, D_MODEL)),
        'w_in': nrm(14, (DEPTH, D_MODEL, IN_COLS), D_MODEL ** -0.5),
        'g_q_lat': gain(15, (DEPTH, Q_LORA)),
        'g_kv_lat': gain(16, (DEPTH, KV_LORA)),
        'w_uq': nrm(17, (DEPTH, Q_LORA, N_HEADS, QK_HEAD), Q_LORA ** -0.5),
        'w_uk': nrm(18, (DEPTH, KV_LORA, N_HEADS, QK_NOPE), KV_LORA ** -0.5),
        'w_uv': nrm(19, (DEPTH, KV_LORA, N_HEADS, V_HEAD), KV_LORA ** -0.5),
        'conv_w': nrm(20, (DEPTH, CONV_WIDTH, CONV_CH), CONV_WIDTH ** -0.5),
        'conv_b': nrm(21, (DEPTH, CONV_CH), 0.01),
        'conv_ln_g': gain(22, (DEPTH, CONV_CH)),
        'conv_ln_b': nrm(23, (DEPTH, CONV_CH), 0.01),
        'w_out': nrm(24, (DEPTH, MIX_WIDTH, D_MODEL), MIX_WIDTH ** -0.5),
        'w_router_group': nrm(25, (DEPTH, D_MODEL, N_EXPERT_GROUPS), D_MODEL ** -0.5),
        'b_router_group': nrm(26, (DEPTH, N_EXPERT_GROUPS), 0.01),
        'w_router_expert': nrm(27, (DEPTH, D_MODEL, N_EXPERTS), D_MODEL ** -0.5),
        'b_router_expert': nrm(28, (DEPTH, N_EXPERTS), 0.01),
        'w_exp_gate': nrm(29, (DEPTH, N_EXPERTS, D_MODEL, D_EXPERT), D_MODEL ** -0.5),
        'w_exp_up': nrm(30, (DEPTH, N_EXPERTS, D_MODEL, D_EXPERT), D_MODEL ** -0.5),
        'w_exp_down': nrm(31, (DEPTH, N_EXPERTS, D_EXPERT, D_MODEL), D_EXPERT ** -0.5),
    }


def reference(x_prompt, x_sample, c_prompt, c_sample, cache_kv_latent, cache_k_rope, state_conv,
              page_table, w_ada, b_ada, g_pre_mix, g_post_mix, g_pre_ffn, g_post_ffn, w_in, g_q_lat,
              g_kv_lat, w_uq, w_uk, w_uv, conv_w, conv_b, conv_ln_g, conv_ln_b, w_out, w_router_group,
              b_router_group, w_router_expert, b_router_expert, w_exp_gate, w_exp_up, w_exp_down):
    pos_prompt = jnp.arange(x_prompt.shape[1])
    pos_sample = PAST_LEN + jnp.arange(x_sample.shape[1])
    hist_prompt = jnp.zeros((x_prompt.shape[0], CONV_WIDTH - 1, CONV_CH), x_prompt.dtype)
    y_prompt, y_sample = x_prompt, x_sample
    kvl_p, kr_p, cv_p, kvl_s, kr_s, cv_s = [], [], [], [], [], []
    for l in range(DEPTH):
        lw = [w[l] for w in (w_ada, b_ada, g_pre_mix, g_post_mix, g_pre_ffn, g_post_ffn, w_in,
                             g_q_lat, g_kv_lat, w_uq, w_uk, w_uv, conv_w, conv_b, conv_ln_g,
                             conv_ln_b, w_out, w_router_group, b_router_group, w_router_expert,
                             b_router_expert, w_exp_gate, w_exp_up, w_exp_down)]
        y_prompt, a_lat, a_rope, a_conv = _decoder_layer(
            y_prompt, c_prompt, pos_prompt, hist_prompt, _attend_prompt, *lw)
        sample_attend = _make_sample_attend(cache_kv_latent[l], cache_k_rope[l], page_table)
        y_sample, s_lat, s_rope, s_conv = _decoder_layer(
            y_sample, c_sample, pos_sample, state_conv[l], sample_attend, *lw)
        kvl_p.append(a_lat)
        kr_p.append(a_rope)
        cv_p.append(a_conv)
        kvl_s.append(s_lat)
        kr_s.append(s_rope)
        cv_s.append(s_conv)
    return (y_prompt, y_sample, jnp.stack(kvl_p), jnp.stack(kr_p), jnp.stack(cv_p),
            jnp.stack(kvl_s), jnp.stack(kr_s), jnp.stack(cv_s))
```

```python
import functools

import jax
import jax.numpy as jnp
from jax import lax
from jax.experimental import pallas as pl
from jax.experimental.pallas import tpu as pltpu

F32 = jnp.float32
BF16 = jnp.bfloat16

V_HEAD = 128
QK_NOPE = 128
QK_ROPE = 64
QK_HEAD = QK_NOPE + QK_ROPE
HEAD_PAD = 256
ROPE_THETA = 10000.0
SOFTMAX_SCALE = QK_HEAD ** -0.5
CONV_WIDTH = 31
N_EXPERT_GROUPS = 4
EXPERTS_PER_GROUP = 8
N_EXPERTS = N_EXPERT_GROUPS * EXPERTS_PER_GROUP
TOP_K = 2
NORM_EPS = 1e-6
LN_EPS = 1e-5
LANES = 128
HALO = 32

MIB = 1024 * 1024


def _cparams(vmem_mib, n_axes):
    return pltpu.CompilerParams(dimension_semantics=("arbitrary",) * n_axes,
                                vmem_limit_bytes=vmem_mib * MIB)


def _sigmoid(x):
    return 1.0 / (1.0 + jnp.exp(-x))


def _rms(x):
    return x * lax.rsqrt(jnp.mean(x * x, axis=-1, keepdims=True) + NORM_EPS)


def _rope_chunk(c, cos, sin):
    lane = lax.broadcasted_iota(jnp.int32, c.shape, 1)
    sw = jnp.where(lane < QK_ROPE // 2, pltpu.roll(c, LANES - QK_ROPE // 2, 1), pltpu.roll(c, QK_ROPE // 2, 1))
    return c * cos + sw * sin


def _ada_kernel(c_ref, w_ref, b_ref, o_ref):
    c = c_ref[...]
    a = (c * _sigmoid(c)).astype(BF16)
    o_ref[...] = jnp.dot(a, w_ref[...].astype(BF16), preferred_element_type=F32) + b_ref[...]


def _ada(c_all, w_ada, b_ada):
    m, d = c_all.shape
    n = w_ada.shape[1]
    tn = 512
    return pl.pallas_call(
        _ada_kernel,
        grid=(n // tn,),
        in_specs=[pl.BlockSpec((m, d), lambda j: (0, 0)),
                  pl.BlockSpec((d, tn), lambda j: (0, j)),
                  pl.BlockSpec((1, tn), lambda j: (0, j))],
        out_specs=pl.BlockSpec((m, tn), lambda j: (0, j)),
        out_shape=jax.ShapeDtypeStruct((m, n), F32),
        compiler_params=_cparams(40, 1),
        name="ada_ln",
    )(c_all, w_ada, b_ada.reshape(1, n))


class _Mod:
    def __init__(self, arr, rows_per_seq, d):
        self.arr = arr
        self.rows_per_seq = rows_per_seq
        self.d = d
        self.per_row = arr.shape[1] != 1

    def spec(self, tm, k, n_grid_axes):
        d = self.d
        if self.per_row:
            if n_grid_axes == 1:
                return pl.BlockSpec((1, tm, d), lambda i: (0, i, k))
            return pl.BlockSpec((1, tm, d), lambda i, j: (0, i, k))
        tiles = self.rows_per_seq // tm
        if n_grid_axes == 1:
            return pl.BlockSpec((1, 1, d), lambda i: (i // tiles, 0, k))
        return pl.BlockSpec((1, 1, d), lambda i, j: (i // tiles, 0, k))


def _premix_kernel(x_ref, g_ref, sc_ref, sh_ref, w_ref, o_ref, h_ref):
    @pl.when(pl.program_id(1) == 0)
    def _():
        h = _rms(x_ref[...]) * g_ref[...]
        h = h * (1.0 + sc_ref[0]) + sh_ref[0]
        h_ref[...] = h.astype(BF16)

    o_ref[...] = jnp.dot(h_ref[...], w_ref[...], preferred_element_type=F32)


def _premix(x, g, mod, w_b):
    t, d = x.shape
    n = w_b.shape[1]
    tm, tn = (128 if mod.per_row else 512), 640
    return pl.pallas_call(
        _premix_kernel,
        grid=(t // tm, n // tn),
        in_specs=[pl.BlockSpec((tm, d), lambda i, j: (i, 0)),
                  pl.BlockSpec((1, d), lambda i, j: (0, 0)),
                  mod.spec(tm, 1, 2), mod.spec(tm, 0, 2),
                  pl.BlockSpec((d, tn), lambda i, j: (0, j))],
        out_specs=pl.BlockSpec((tm, tn), lambda i, j: (i, j)),
        out_shape=jax.ShapeDtypeStruct((t, n), F32),
        scratch_shapes=[pltpu.VMEM((tm, d), BF16)],
        compiler_params=_cparams(48, 2),
        name="premix_in_proj",
    )(x, g.reshape(1, d), mod.arr, mod.arr, w_b)


def _kv_kernel(with_up, n_heads, kva_ref, kr_ref, g_ref, cos_ref, sin_ref, *rest):
    if with_up:
        wuk_ref, wuv_ref, kv_out, kr_out, kfull_out, v_out = rest
    else:
        kv_out, kr_out = rest
    kv = _rms(kva_ref[...]) * g_ref[...]
    kv_out[...] = kv
    rot = _rope_chunk(kr_ref[...], cos_ref[...], sin_ref[...])
    kr_out[...] = rot[:, :QK_ROPE]
    if with_up:
        kvb = kv.astype(BF16)
        k_nope = jnp.dot(kvb, wuk_ref[...], preferred_element_type=F32)
        v_out[...] = jnp.dot(kvb, wuv_ref[...], preferred_element_type=F32).astype(BF16)
        rot_b = rot.astype(BF16)
        for h in range(n_heads):
            kfull_out[:, h * HEAD_PAD:h * HEAD_PAD + QK_NOPE] = k_nope[:, h * QK_NOPE:(h + 1) * QK_NOPE].astype(BF16)
            kfull_out[:, h * HEAD_PAD + QK_NOPE:(h + 1) * HEAD_PAD] = rot_b


def _kv_post(z, g_kv, cos_t, sin_t, pos_rows, kv_col, kr_col, wuk_b=None, wuv_b=None):
    t = z.shape[0]
    kv_lora = g_kv.shape[0]
    tm = 256
    pos_tiles = pos_rows // tm
    with_up = wuk_b is not None
    in_specs = [pl.BlockSpec((tm, kv_lora), lambda i: (i, kv_col)),
                pl.BlockSpec((tm, LANES), lambda i: (i, kr_col)),
                pl.BlockSpec((1, kv_lora), lambda i: (0, 0)),
                pl.BlockSpec((tm, LANES), lambda i: (i % pos_tiles, 0)),
                pl.BlockSpec((tm, LANES), lambda i: (i % pos_tiles, 0))]
    args = [z, z, g_kv.reshape(1, kv_lora), cos_t, sin_t]
    out_specs = [pl.BlockSpec((tm, kv_lora), lambda i: (i, 0)),
                 pl.BlockSpec((tm, QK_ROPE), lambda i: (i, 0))]
    out_shape = [jax.ShapeDtypeStruct((t, kv_lora), F32), jax.ShapeDtypeStruct((t, QK_ROPE), F32)]
    n_heads = 0
    if with_up:
        n_heads = wuk_b.shape[1] // QK_NOPE
        in_specs += [pl.BlockSpec(wuk_b.shape, lambda i: (0, 0)), pl.BlockSpec(wuv_b.shape, lambda i: (0, 0))]
        args += [wuk_b, wuv_b]
        out_specs += [pl.BlockSpec((tm, n_heads * HEAD_PAD), lambda i: (i, 0)),
                      pl.BlockSpec((tm, n_heads * V_HEAD), lambda i: (i, 0))]
        out_shape += [jax.ShapeDtypeStruct((t, n_heads * HEAD_PAD), BF16),
                      jax.ShapeDtypeStruct((t, n_heads * V_HEAD), BF16)]
    return pl.pallas_call(
        functools.partial(_kv_kernel, with_up, n_heads),
        grid=(t // tm,),
        in_specs=in_specs, out_specs=out_specs, out_shape=out_shape,
        compiler_params=_cparams(32, 1),
        name="kv_post_up" if with_up else "kv_post",
    )(*args)


def _q_kernel(heads_per_step, qa_ref, g_ref, cos_ref, sin_ref, w_ref, o_ref):
    qn = (_rms(qa_ref[...]) * g_ref[...]).astype(BF16)
    q = jnp.dot(qn, w_ref[...], preferred_element_type=F32)
    cos = cos_ref[...]
    sin = sin_ref[...]
    for h in range(heads_per_step):
        lo = h * HEAD_PAD
        o_ref[:, lo:lo + QK_NOPE] = q[:, lo:lo + QK_NOPE].astype(BF16)
        o_ref[:, lo + QK_NOPE:lo + HEAD_PAD] = _rope_chunk(q[:, lo + QK_NOPE:lo + HEAD_PAD], cos, sin).astype(BF16)


def _q_proj(z, g_q, cos_t, sin_t, pos_rows, qa_col, wq_b):
    t = z.shape[0]
    q_lora, n = wq_b.shape
    tm, tn = 256, 1024
    pos_tiles = pos_rows // tm
    return pl.pallas_call(
        functools.partial(_q_kernel, tn // HEAD_PAD),
        grid=(t // tm, n // tn),
        in_specs=[pl.BlockSpec((tm, q_lora), lambda i, j: (i, qa_col)),
                  pl.BlockSpec((1, q_lora), lambda i, j: (0, 0)),
                  pl.BlockSpec((tm, LANES), lambda i, j: (i % pos_tiles, 0)),
                  pl.BlockSpec((tm, LANES), lambda i, j: (i % pos_tiles, 0)),
                  pl.BlockSpec((q_lora, tn), lambda i, j: (0, j))],
        out_specs=pl.BlockSpec((tm, tn), lambda i, j: (i, j)),
        out_shape=jax.ShapeDtypeStruct((t, n), BF16),
        compiler_params=_cparams(32, 2),
        name="q_proj_rope",
    )(z, g_q.reshape(1, q_lora), cos_t, sin_t, wq_b)


def _flash_kernel(tq, tk, nk, q_ref, k_ref, v_ref, o_ref, m_ref, l_ref, acc_ref):
    i = pl.program_id(2)
    j = pl.program_id(3)

    @pl.when(j == 0)
    def _():
        m_ref[...] = jnp.full(m_ref.shape, -jnp.inf, F32)
        l_ref[...] = jnp.zeros(l_ref.shape, F32)
        acc_ref[...] = jnp.zeros(acc_ref.shape, F32)

    @pl.when(j <= i)
    def _():
        s = lax.dot_general(q_ref[...], k_ref[...], (((1,), (1,)), ((), ())),
                            preferred_element_type=F32) * SOFTMAX_SCALE
        row = i * tq + lax.broadcasted_iota(jnp.int32, s.shape, 0)
        col = j * tk + lax.broadcasted_iota(jnp.int32, s.shape, 1)
        s = jnp.where(col <= row, s, -jnp.inf)
        m_prev = m_ref[...]
        m_new = jnp.maximum(m_prev, jnp.max(s, axis=-1, keepdims=True))
        alpha = jnp.exp(m_prev - m_new)
        p = jnp.exp(s - m_new)
        l_ref[...] = alpha * l_ref[...] + jnp.sum(p, axis=-1, keepdims=True)
        acc_ref[...] = alpha * acc_ref[...] + jnp.dot(p.astype(BF16), v_ref[...], preferred_element_type=F32)
        m_ref[...] = m_new

    @pl.when(j == nk - 1)
    def _():
        o_ref[...] = (acc_ref[...] / l_ref[...]).astype(o_ref.dtype)


def _flash(q_full, k_full, v, n_heads):
    b, t, _ = q_full.shape
    tq = tk = 512
    nq, nk = t // tq, t // tk
    return pl.pallas_call(
        functools.partial(_flash_kernel, tq, tk, nk),
        grid=(b, n_heads, nq, nk),
        in_specs=[pl.BlockSpec((None, tq, HEAD_PAD), lambda bb, h, i, j: (bb, i, h)),
                  pl.BlockSpec((None, tk, HEAD_PAD), lambda bb, h, i, j: (bb, jnp.minimum(j, i), h)),
                  pl.BlockSpec((None, tk, V_HEAD), lambda bb, h, i, j: (bb, jnp.minimum(j, i), h))],
        out_specs=pl.BlockSpec((None, tq, V_HEAD), lambda bb, h, i, j: (bb, i, h)),
        out_shape=jax.ShapeDtypeStruct((b, t, n_heads * V_HEAD), BF16),
        scratch_shapes=[pltpu.VMEM((tq, 1), F32), pltpu.VMEM((tq, 1), F32), pltpu.VMEM((tq, V_HEAD), F32)],
        compiler_params=_cparams(32, 4),
        name="prompt_flash_attn",
    )(q_full, k_full, v)


def _absorb_kernel(q_ref, w_ref, ql_ref, qr_ref):
    q = q_ref[...]
    ql_ref[...] = jnp.dot(q[:, :QK_NOPE], w_ref[...], preferred_element_type=F32).astype(BF16)
    qr_ref[...] = q[:, QK_NOPE:QK_NOPE + QK_ROPE]


def _absorb(q_full, wukT_b):
    t = q_full.shape[0]
    n_heads, _, kv_lora = wukT_b.shape
    return pl.pallas_call(
        _absorb_kernel,
        grid=(n_heads,),
        in_specs=[pl.BlockSpec((t, HEAD_PAD), lambda h: (0, h)),
                  pl.BlockSpec((None, QK_NOPE, kv_lora), lambda h: (h, 0, 0))],
        out_specs=[pl.BlockSpec((t, kv_lora), lambda h: (0, h)),
                   pl.BlockSpec((None, t, QK_ROPE), lambda h: (h, 0, 0))],
        out_shape=[jax.ShapeDtypeStruct((t, n_heads * kv_lora), BF16),
                   jax.ShapeDtypeStruct((n_heads, t, QK_ROPE), BF16)],
        compiler_params=_cparams(32, 1),
        name="sample_absorb_q",
    )(q_full, wukT_b)


def _paged_kernel(pages, page_size, n_chunks, n_heads, pt_ref, ql_ref, qr_ref, kvn_ref, krn_ref, *rest):
    kv_refs = rest[:pages]
    kr_refs = rest[pages:2 * pages]
    o_ref, kbuf, rbuf, m_ref, l_ref, acc_ref = rest[2 * pages:]
    c = pl.program_id(1)

    @pl.when(c == 0)
    def _():
        m_ref[...] = jnp.full(m_ref.shape, -jnp.inf, F32)
        l_ref[...] = jnp.zeros(l_ref.shape, F32)
        acc_ref[...] = jnp.zeros(acc_ref.shape, F32)

    for p in range(pages):
        kbuf[p * page_size:(p + 1) * page_size, :] = kv_refs[p][...].astype(BF16)
        rbuf[p * page_size:(p + 1) * page_size, :] = kr_refs[p][...].astype(BF16)

    ql = ql_ref[...]
    qr = qr_ref[...]
    nt = (((1,), (1,)), ((), ()))
    s = (lax.dot_general(ql, kbuf[...], nt, preferred_element_type=F32)
         + lax.dot_general(qr, rbuf[...], nt, preferred_element_type=F32)) * SOFTMAX_SCALE
    m_prev = m_ref[...]
    m_new = jnp.maximum(m_prev, jnp.max(s, axis=-1, keepdims=True))
    alpha = jnp.exp(m_prev - m_new)
    p_ = jnp.exp(s - m_new)
    l_ref[...] = alpha * l_ref[...] + jnp.sum(p_, axis=-1, keepdims=True)
    acc_ref[...] = alpha * acc_ref[...] + jnp.dot(p_.astype(BF16), kbuf[...], preferred_element_type=F32)
    m_ref[...] = m_new

    @pl.when(c == n_chunks - 1)
    def _():
        qlf = ql.astype(F32)
        qrf = qr.astype(F32)
        kvn = kvn_ref[...].astype(BF16).astype(F32)
        krn = krn_ref[...].astype(BF16).astype(F32)
        n_new = kvn.shape[0]
        tok = lax.broadcasted_iota(jnp.int32, (ql.shape[0], 1), 0) // n_heads
        s_new = []
        for jn in range(n_new):
            sj = (jnp.sum(qlf * kvn[jn:jn + 1, :], axis=-1, keepdims=True)
                  + jnp.sum(qrf * krn[jn:jn + 1, :], axis=-1, keepdims=True)) * SOFTMAX_SCALE
            s_new.append(jnp.where(tok >= jn, sj, -jnp.inf))
        m_prev2 = m_ref[...]
        m_fin = m_prev2
        for sj in s_new:
            m_fin = jnp.maximum(m_fin, sj)
        alpha2 = jnp.exp(m_prev2 - m_fin)
        l_fin = alpha2 * l_ref[...]
        acc = alpha2 * acc_ref[...]
        for jn, sj in enumerate(s_new):
            pj = jnp.exp(sj - m_fin)
            l_fin = l_fin + pj
            acc = acc + pj.astype(BF16).astype(F32) * kvn[jn:jn + 1, :]
        o_ref[...] = (acc / l_fin).astype(o_ref.dtype)


def _paged_attn(page_table, q_lat, q_rope, kv_new, kr_new, cache_kv, cache_kr, n_heads):
    n_seq, rows, kv_lora = q_lat.shape
    page_size = cache_kv.shape[2]
    n_pages = page_table.shape[1]
    n_new = kv_new.shape[1]
    pages = 16
    n_chunks = n_pages // pages

    def page_spec(p, width):
        return pl.BlockSpec((None, None, page_size, width),
                            lambda b, c, pt: (0, pt[b, c * pages + p], 0, 0))

    in_specs = [pl.BlockSpec((None, rows, kv_lora), lambda b, c, pt: (b, 0, 0)),
                pl.BlockSpec((None, rows, QK_ROPE), lambda b, c, pt: (b, 0, 0)),
                pl.BlockSpec((None, n_new, kv_lora), lambda b, c, pt: (b, 0, 0)),
                pl.BlockSpec((None, n_new, QK_ROPE), lambda b, c, pt: (b, 0, 0))]
    in_specs += [page_spec(p, kv_lora) for p in range(pages)]
    in_specs += [page_spec(p, QK_ROPE) for p in range(pages)]
    grid_spec = pltpu.PrefetchScalarGridSpec(
        num_scalar_prefetch=1,
        grid=(n_seq, n_chunks),
        in_specs=in_specs,
        out_specs=pl.BlockSpec((None, rows, kv_lora), lambda b, c, pt: (b, 0, 0)),
        scratch_shapes=[pltpu.VMEM((pages * page_size, kv_lora), BF16),
                        pltpu.VMEM((pages * page_size, QK_ROPE), BF16),
                        pltpu.VMEM((rows, 1), F32), pltpu.VMEM((rows, 1), F32),
                        pltpu.VMEM((rows, kv_lora), F32)])
    return pl.pallas_call(
        functools.partial(_paged_kernel, pages, page_size, n_chunks, n_heads),
        grid_spec=grid_spec,
        out_shape=jax.ShapeDtypeStruct((n_seq, rows, kv_lora), BF16),
        compiler_params=_cparams(40, 2),
        name="sample_paged_attn",
    )(page_table, q_lat, q_rope, kv_new, kr_new, *([cache_kv] * pages), *([cache_kr] * pages))


def _vup_kernel(o_ref, w_ref, a_ref):
    a_ref[...] = jnp.dot(o_ref[...], w_ref[...], preferred_element_type=F32).astype(BF16)


def _v_up(o_lat, wuv_h_b):
    t = o_lat.shape[0]
    n_heads, kv_lora, _ = wuv_h_b.shape
    return pl.pallas_call(
        _vup_kernel,
        grid=(n_heads,),
        in_specs=[pl.BlockSpec((t, kv_lora), lambda h: (0, h)),
                  pl.BlockSpec((None, kv_lora, V_HEAD), lambda h: (h, 0, 0))],
        out_specs=pl.BlockSpec((t, V_HEAD), lambda h: (0, h)),
        out_shape=jax.ShapeDtypeStruct((t, n_heads * V_HEAD), BF16),
        compiler_params=_cparams(32, 1),
        name="sample_v_up",
    )(o_lat, wuv_h_b)


def _ln_swish(y, g, b):
    mu = jnp.mean(y, axis=-1, keepdims=True)
    yc = y - mu
    var = jnp.mean(yc * yc, axis=-1, keepdims=True)
    o = yc * lax.rsqrt(var + LN_EPS) * g + b
    return o * _sigmoid(o)


def _conv_prompt_kernel(tt, cw, n_cc, val_ref, gate_ref, hval_ref, hgate_ref, w_ref, b_ref, g_ref, lb_ref,
                        o_ref, st_ref, ext_ref, y_ref):
    i = pl.program_id(1)
    cc = pl.program_id(2)
    um = val_ref[...] * _sigmoid(gate_ref[...])
    uh = hval_ref[...] * _sigmoid(hgate_ref[...])
    uh = jnp.where(i == 0, 0.0, uh)
    ext_ref[0:HALO, :] = uh
    ext_ref[HALO:HALO + tt, :] = um
    st_ref[cc] = um[tt - HALO:, :]
    rb = 64
    w = w_ref[...]
    bias = b_ref[...]
    for r in range(0, tt, rb):
        acc = jnp.zeros((rb, cw), F32) + bias
        for k in range(CONV_WIDTH):
            s0 = r + k + HALO - (CONV_WIDTH - 1)
            acc = acc + w[k:k + 1, :] * ext_ref[s0:s0 + rb, :]
        y_ref[cc, r:r + rb, :] = acc

    @pl.when(cc == n_cc - 1)
    def _():
        s1 = jnp.zeros((tt, 1), F32)
        for c2 in range(n_cc):
            s1 = s1 + jnp.sum(y_ref[c2], axis=-1, keepdims=True)
        mu = s1 / (n_cc * cw)
        s2 = jnp.zeros((tt, 1), F32)
        for c2 in range(n_cc):
            d = y_ref[c2] - mu
            s2 = s2 + jnp.sum(d * d, axis=-1, keepdims=True)
        rs = lax.rsqrt(s2 / (n_cc * cw) + LN_EPS)
        for c2 in range(n_cc):
            o = (y_ref[c2] - mu) * rs * g_ref[:, c2 * cw:(c2 + 1) * cw] + lb_ref[:, c2 * cw:(c2 + 1) * cw]
            o_ref[:, c2 * cw:(c2 + 1) * cw] = (o * _sigmoid(o)).astype(o_ref.dtype)


def _conv_prompt(z, n_seq, seq, conv_ch, val_col0, conv_w, conv_b, ln_g, ln_b):
    tt, cw = 256, 256
    n_cc = conv_ch // cw
    nt = seq // tt
    hb = tt // HALO
    gate_off = conv_ch // cw

    def main(off):
        return pl.BlockSpec((tt, cw), lambda b, i, c: (b * nt + i, val_col0 + off + c))

    def halo(off):
        return pl.BlockSpec((HALO, cw), lambda b, i, c: (jnp.maximum((b * nt + i) * hb - 1, 0), val_col0 + off + c))

    return pl.pallas_call(
        functools.partial(_conv_prompt_kernel, tt, cw, n_cc),
        grid=(n_seq, nt, n_cc),
        in_specs=[main(0), main(gate_off), halo(0), halo(gate_off),
                  pl.BlockSpec((CONV_WIDTH, cw), lambda b, i, c: (0, c)),
                  pl.BlockSpec((1, cw), lambda b, i, c: (0, c)),
                  pl.BlockSpec((1, conv_ch), lambda b, i, c: (0, 0)),
                  pl.BlockSpec((1, conv_ch), lambda b, i, c: (0, 0))],
        out_specs=[pl.BlockSpec((tt, conv_ch), lambda b, i, c: (b * nt + i, 0)),
                   pl.BlockSpec((None, n_cc, HALO, cw), lambda b, i, c: (b, 0, 0, 0))],
        out_shape=[jax.ShapeDtypeStruct((n_seq * seq, conv_ch), BF16),
                   jax.ShapeDtypeStruct((n_seq, n_cc, HALO, cw), F32)],
        scratch_shapes=[pltpu.VMEM((HALO + tt, cw), F32), pltpu.VMEM((n_cc, tt, cw), F32)],
        compiler_params=_cparams(32, 3),
        name="conv_prompt",
    )(z, z, z, z, conv_w, conv_b.reshape(1, conv_ch), ln_g.reshape(1, conv_ch), ln_b.reshape(1, conv_ch))


def _conv_sample_kernel(n_new, st_ref, val_ref, gate_ref, wst_ref, wu_ref, b_ref, g_ref, lb_ref, o_ref, u_ref):
    u = val_ref[...] * _sigmoid(gate_ref[...])
    u_ref[...] = u
    st = st_ref[...]
    for t in range(n_new):
        y = (jnp.sum(st * wst_ref[t][None], axis=1) + jnp.sum(u * wu_ref[t][None], axis=1) + b_ref[...])
        o_ref[t] = _ln_swish(y, g_ref[...], lb_ref[...]).astype(o_ref.dtype)


def _conv_sample(state, z3, conv_ch, val_col0, conv_w, conv_b, ln_g, ln_b):
    n_seq, hist, _ = state.shape
    n_new = z3.shape[1]
    nb = 16
    jj = jnp.arange(hist)[None, :] - jnp.arange(n_new)[:, None]
    wst = jnp.where((jj >= 0)[..., None], conv_w[jnp.clip(jj, 0, CONV_WIDTH - 1)], 0.0)
    ii = hist - jnp.arange(n_new)[:, None] + jnp.arange(n_new)[None, :]
    wu = jnp.where((ii <= CONV_WIDTH - 1)[..., None], conv_w[jnp.clip(ii, 0, CONV_WIDTH - 1)], 0.0)
    gate_off = 1
    return pl.pallas_call(
        functools.partial(_conv_sample_kernel, n_new),
        grid=(n_seq // nb,),
        in_specs=[pl.BlockSpec((nb, hist, conv_ch), lambda i: (i, 0, 0)),
                  pl.BlockSpec((nb, n_new, conv_ch), lambda i: (i, 0, val_col0)),
                  pl.BlockSpec((nb, n_new, conv_ch), lambda i: (i, 0, val_col0 + gate_off)),
                  pl.BlockSpec((n_new, hist, conv_ch), lambda i: (0, 0, 0)),
                  pl.BlockSpec((n_new, n_new, conv_ch), lambda i: (0, 0, 0)),
                  pl.BlockSpec((1, conv_ch), lambda i: (0, 0)),
                  pl.BlockSpec((1, conv_ch), lambda i: (0, 0)),
                  pl.BlockSpec((1, conv_ch), lambda i: (0, 0))],
        out_specs=[pl.BlockSpec((n_new, nb, conv_ch), lambda i: (0, i, 0)),
                   pl.BlockSpec((nb, n_new, conv_ch), lambda i: (i, 0, 0))],
        out_shape=[jax.ShapeDtypeStruct((n_new, n_seq, conv_ch), BF16),
                   jax.ShapeDtypeStruct((n_seq, n_new, conv_ch), F32)],
        compiler_params=_cparams(32, 1),
        name="conv_sample",
    )(state, z3, z3, wst, wu, conv_b.reshape(1, conv_ch), ln_g.reshape(1, conv_ch), ln_b.reshape(1, conv_ch))


def _pack_pair(lo, hi):
    lo_u = lax.bitcast_convert_type(lo.astype(BF16).astype(F32), jnp.uint32)
    hi_u = lax.bitcast_convert_type(hi.astype(BF16).astype(F32), jnp.uint32)
    return (lo_u >> 16) | (hi_u & jnp.uint32(0xFFFF0000))


def _unpack_pair(w):
    lo = lax.bitcast_convert_type(w << 16, F32).astype(BF16)
    hi = lax.bitcast_convert_type(w & jnp.uint32(0xFFFF0000), F32).astype(BF16)
    return lo, hi


def _outproj_kernel(nj, tn, a_ref, c_ref, wt_ref, wb_ref, x_ref, ga_ref, gpost_ref, gpre_ref, sc_ref, sh_ref,
                    wr_ref, br_ref, x1_ref, hp_ref, lg_ref, mix_ref):
    j = pl.program_id(1)
    mix_ref[j] = (jnp.dot(a_ref[...], wt_ref[...], preferred_element_type=F32)
                  + jnp.dot(c_ref[...], wb_ref[...], preferred_element_type=F32))

    @pl.when(j == nj - 1)
    def _():
        tm = x_ref.shape[0]
        d = nj * tn
        ss = jnp.zeros((tm, 1), F32)
        for c in range(nj):
            m = mix_ref[c]
            ss = ss + jnp.sum(m * m, axis=-1, keepdims=True)
        r = lax.rsqrt(ss / d + NORM_EPS)
        ss1 = jnp.zeros((tm, 1), F32)
        for c in range(nj):
            cs = slice(c * tn, (c + 1) * tn)
            x1 = x_ref[:, cs] + ga_ref[0, :, cs] * (mix_ref[c] * r * gpost_ref[:, cs])
            x1_ref[:, cs] = x1
            ss1 = ss1 + jnp.sum(x1 * x1, axis=-1, keepdims=True)
        r1 = lax.rsqrt(ss1 / d + NORM_EPS)
        half = nj // 2
        lg = jnp.zeros(lg_ref.shape, F32) + br_ref[...]
        for c in range(half):
            parts = []
            for cc in (c, c + half):
                cs = slice(cc * tn, (cc + 1) * tn)
                h = (x1_ref[:, cs] * r1 * gpre_ref[:, cs]) * (1.0 + sc_ref[0, :, cs]) + sh_ref[0, :, cs]
                lg = lg + jnp.dot(h.astype(BF16), wr_ref[cs, :], preferred_element_type=F32)
                parts.append(h)
            hp_ref[:, c * tn:(c + 1) * tn] = _pack_pair(parts[0], parts[1])
        lg_ref[...] = lg


def _outproj(attn, conv, w_out_b, x, mod, g_post, g_pre, wr_b, br):
    t, d = x.shape
    kh = attn.shape[1]
    tm, tn = (128 if mod.per_row else 256), 512
    nj = d // tn
    return pl.pallas_call(
        functools.partial(_outproj_kernel, nj, tn),
        grid=(t // tm, nj),
        in_specs=[pl.BlockSpec((tm, kh), lambda i, j: (i, 0)),
                  pl.BlockSpec((tm, kh), lambda i, j: (i, 0)),
                  pl.BlockSpec((kh, tn), lambda i, j: (0, j)),
                  pl.BlockSpec((kh, tn), lambda i, j: (1, j)),
                  pl.BlockSpec((tm, d), lambda i, j: (i, 0)),
                  mod.spec(tm, 2, 2),
                  pl.BlockSpec((1, d), lambda i, j: (0, 0)),
                  pl.BlockSpec((1, d), lambda i, j: (0, 0)),
                  mod.spec(tm, 4, 2), mod.spec(tm, 3, 2),
                  pl.BlockSpec((d, LANES), lambda i, j: (0, 0)),
                  pl.BlockSpec((1, LANES), lambda i, j: (0, 0))],
        out_specs=[pl.BlockSpec((tm, d), lambda i, j: (i, 0)),
                   pl.BlockSpec((tm, d // 2), lambda i, j: (i, 0)),
                   pl.BlockSpec((tm, LANES), lambda i, j: (i, 0))],
        out_shape=[jax.ShapeDtypeStruct((t, d), F32),
                   jax.ShapeDtypeStruct((t, d // 2), jnp.uint32),
                   jax.ShapeDtypeStruct((t, LANES), F32)],
        scratch_shapes=[pltpu.VMEM((nj, tm, tn), F32)],
        compiler_params=_cparams(48, 2),
        name="out_proj_residual_prenorm",
    )(attn, conv, w_out_b, w_out_b, x, mod.arr, g_post.reshape(1, d), g_pre.reshape(1, d), mod.arr, mod.arr,
      wr_b, br)


def _router_kernel(lg_ref, id_ref, wt_ref):
    lg = lg_ref[...]
    lane = lax.broadcasted_iota(jnp.int32, lg.shape, 1)
    is_g = lane < N_EXPERT_GROUPS
    gmax = jnp.max(jnp.where(is_g, lg, -jnp.inf), axis=-1, keepdims=True)
    gexp = jnp.where(is_g, jnp.exp(jnp.where(is_g, lg, gmax) - gmax), 0.0)
    gprob = gexp / jnp.sum(gexp, axis=-1, keepdims=True)
    gw = jnp.max(gprob, axis=-1, keepdims=True)
    gidx = jnp.min(jnp.where(is_g & (gprob == gw), lane, LANES), axis=-1, keepdims=True)
    lo = N_EXPERT_GROUPS + gidx * EXPERTS_PER_GROUP
    is_e = (lane >= lo) & (lane < lo + EXPERTS_PER_GROUP)
    emax = jnp.max(jnp.where(is_e, lg, -jnp.inf), axis=-1, keepdims=True)
    eexp = jnp.where(is_e, jnp.exp(jnp.where(is_e, lg, emax) - emax), 0.0)
    ep = jnp.where(is_e, eexp / jnp.sum(eexp, axis=-1, keepdims=True), -1.0)
    p1 = jnp.max(ep, axis=-1, keepdims=True)
    i1 = jnp.min(jnp.where(ep == p1, lane, LANES), axis=-1, keepdims=True)
    ep2 = jnp.where(lane == i1, -1.0, ep)
    p2 = jnp.max(ep2, axis=-1, keepdims=True)
    i2 = jnp.min(jnp.where(ep2 == p2, lane, LANES), axis=-1, keepdims=True)
    den = p1 + p2
    w1 = p1 / den * gw
    w2 = p2 / den * gw
    id_ref[...] = jnp.where(lane == 0, i1 - N_EXPERT_GROUPS, jnp.where(lane == 1, i2 - N_EXPERT_GROUPS, 0))
    wt_ref[...] = jnp.where(lane == 0, w1, jnp.where(lane == 1, w2, 0.0))


def _router(logits):
    t = logits.shape[0]
    tm = 512
    spec = pl.BlockSpec((tm, LANES), lambda i: (i, 0))
    return pl.pallas_call(
        _router_kernel,
        grid=(t // tm,),
        in_specs=[spec], out_specs=[spec, spec],
        out_shape=[jax.ShapeDtypeStruct((t, LANES), jnp.int32), jax.ShapeDtypeStruct((t, LANES), F32)],
        compiler_params=_cparams(32, 1),
        name="router_topk",
    )(logits)


GATHER_WINDOW = 64


def _gather_kernel(idx_ref, n_ref, src_ref, dst_ref, sem):
    n = n_ref[0]

    def row_copy(r):
        return pltpu.make_async_copy(src_ref.at[pl.ds(idx_ref[r], 1)], dst_ref.at[pl.ds(r, 1)], sem)

    def start_block(blk):
        def body(k, carry):
            row_copy(blk * GATHER_WINDOW + k).start()
            return carry
        lax.fori_loop(0, GATHER_WINDOW, body, 0)

    def wait_block(blk):
        def body(k, carry):
            row_copy(blk * GATHER_WINDOW + k).wait()
            return carry
        lax.fori_loop(0, GATHER_WINDOW, body, 0)

    n_blk = n // GATHER_WINDOW

    @pl.when(n_blk > 0)
    def _():
        start_block(0)

        def body(blk, carry):
            start_block(blk)
            wait_block(blk - 1)
            return carry
        lax.fori_loop(1, n_blk, body, 0)
        wait_block(n_blk - 1)


def _gather_rows(idx, n_rows, src, out_rows):
    return pl.pallas_call(
        _gather_kernel,
        grid_spec=pltpu.PrefetchScalarGridSpec(
            num_scalar_prefetch=2, grid=(1,),
            in_specs=[pl.BlockSpec(memory_space=pl.ANY)],
            out_specs=pl.BlockSpec(memory_space=pl.ANY),
            scratch_shapes=[pltpu.SemaphoreType.DMA(())]),
        out_shape=jax.ShapeDtypeStruct((out_rows, src.shape[1]), src.dtype),
        compiler_params=_cparams(32, 1),
        name="row_gather",
    )(idx, n_rows, src)


def _moe_up_kernel(te_ref, tb_ref, tf_ref, na_ref, xs_ref, wg_ref, wu_ref, h_ref, wg_b, wu_b):
    t = pl.program_id(1)

    @pl.when(tf_ref[t] == 1)
    def _():
        wg_b[...] = wg_ref[...].astype(BF16)
        wu_b[...] = wu_ref[...].astype(BF16)

    @pl.when(t < na_ref[0])
    def _():
        lo, hi = _unpack_pair(xs_ref[...])
        kh = lo.shape[1]
        g = (jnp.dot(lo, wg_b[:kh, :], preferred_element_type=F32)
             + jnp.dot(hi, wg_b[kh:, :], preferred_element_type=F32))
        u = (jnp.dot(lo, wu_b[:kh, :], preferred_element_type=F32)
             + jnp.dot(hi, wu_b[kh:, :], preferred_element_type=F32))
        h_ref[...] = (g * _sigmoid(g) * u).astype(h_ref.dtype)


def _moe_up(tile_expert, tile_blk, tile_first, n_active, xs, w_gate, w_up, tm):
    s, kh = xs.shape
    _, n_exp, d, f = w_gate.shape
    tf = 512
    nt = s // tm
    grid_spec = pltpu.PrefetchScalarGridSpec(
        num_scalar_prefetch=4,
        grid=(f // tf, nt),
        in_specs=[pl.BlockSpec((tm, kh), lambda c, t, te, tb, tfi, na: (tb[t], 0)),
                  pl.BlockSpec((None, None, d, tf), lambda c, t, te, tb, tfi, na: (0, te[t], 0, c)),
                  pl.BlockSpec((None, None, d, tf), lambda c, t, te, tb, tfi, na: (0, te[t], 0, c))],
        out_specs=pl.BlockSpec((tm, tf), lambda c, t, te, tb, tfi, na: (tb[t], c)),
        scratch_shapes=[pltpu.VMEM((d, tf), BF16), pltpu.VMEM((d, tf), BF16)])
    return pl.pallas_call(
        _moe_up_kernel,
        grid_spec=grid_spec,
        out_shape=jax.ShapeDtypeStruct((s, f), BF16),
        compiler_params=_cparams(56, 2),
        name="moe_gate_up",
    )(tile_expert, tile_blk, tile_first, n_active, xs, w_gate, w_up)


def _moe_down_kernel(te_ref, tb_ref, tf_ref, na_ref, h_ref, wd_ref, y_ref, wd_b):
    t = pl.program_id(1)

    @pl.when(tf_ref[t] == 1)
    def _():
        wd_b[...] = wd_ref[...].astype(BF16)

    @pl.when(t < na_ref[0])
    def _():
        y_ref[...] = jnp.dot(h_ref[...], wd_b[...], preferred_element_type=F32)


def _moe_down(tile_expert, tile_blk, tile_first, n_active, h, w_down, tm):
    s, f = h.shape
    d = w_down.shape[3]
    tn = 1024
    nt = s // tm
    grid_spec = pltpu.PrefetchScalarGridSpec(
        num_scalar_prefetch=4,
        grid=(d // tn, nt),
        in_specs=[pl.BlockSpec((tm, f), lambda c, t, te, tb, tfi, na: (tb[t], 0)),
                  pl.BlockSpec((None, None, f, tn), lambda c, t, te, tb, tfi, na: (0, te[t], 0, c))],
        out_specs=pl.BlockSpec((tm, tn), lambda c, t, te, tb, tfi, na: (tb[t], c)),
        scratch_shapes=[pltpu.VMEM((f, tn), BF16)])
    return pl.pallas_call(
        _moe_down_kernel,
        grid_spec=grid_spec,
        out_shape=jax.ShapeDtypeStruct((s, d), F32),
        compiler_params=_cparams(40, 2),
        name="moe_down",
    )(tile_expert, tile_blk, tile_first, n_active, h, w_down)


def _final_kernel(x1_ref, y0_ref, y1_ref, wt_ref, ga_ref, g_ref, o_ref):
    wt = wt_ref[...]
    f = wt[:, 0:1] * y0_ref[...] + wt[:, 1:2] * y1_ref[...]
    o_ref[...] = x1_ref[...] + ga_ref[0] * (_rms(f) * g_ref[...])


def _final(x1, yg, wts, mod, g_post, row0, t_all):
    t, d = x1.shape
    tm = 256
    o0 = row0 // tm
    o1 = (t_all + row0) // tm
    return pl.pallas_call(
        _final_kernel,
        grid=(t // tm,),
        in_specs=[pl.BlockSpec((tm, d), lambda i: (i, 0)),
                  pl.BlockSpec((tm, d), lambda i: (o0 + i, 0)),
                  pl.BlockSpec((tm, d), lambda i: (o1 + i, 0)),
                  pl.BlockSpec((tm, LANES), lambda i: (o0 + i, 0)),
                  mod.spec(tm, 5, 1),
                  pl.BlockSpec((1, d), lambda i: (0, 0))],
        out_specs=pl.BlockSpec((tm, d), lambda i: (i, 0)),
        out_shape=jax.ShapeDtypeStruct((t, d), F32),
        compiler_params=_cparams(48, 1),
        name="combine_residual",
    )(x1, yg, yg, wts, mod.arr, g_post.reshape(1, d))


def _routing_tables(eid, n_tok, tm, nt_max):
    e_flat = eid.reshape(-1)
    onehot = (e_flat[:, None] == jnp.arange(N_EXPERTS, dtype=jnp.int32)[None, :]).astype(jnp.int32)
    csum = jnp.cumsum(onehot, axis=0)
    rank = jnp.sum((csum - onehot) * onehot, axis=1)
    counts = csum[-1]
    tiles_e = (counts + tm - 1) // tm
    tile_end = jnp.cumsum(tiles_e)
    tile_start = tile_end - tiles_e
    n_active = tile_end[-1]
    slot = tile_start[e_flat] * tm + rank
    tile_ids = jnp.minimum(jnp.arange(nt_max, dtype=jnp.int32), n_active - 1)
    tile_expert = jnp.searchsorted(tile_end, tile_ids, side="right").astype(jnp.int32)
    tile_first = jnp.concatenate([jnp.ones((1,), jnp.int32),
                                  (tile_expert[1:] != tile_expert[:-1]).astype(jnp.int32)])
    tok = jnp.arange(2 * n_tok, dtype=jnp.int32) // 2
    src = jnp.zeros((nt_max * tm,), jnp.int32).at[slot].set(tok)
    pos = slot.reshape(n_tok, 2).T.reshape(-1)
    return (tile_expert, tile_ids.astype(jnp.int32), tile_first, n_active.reshape(1).astype(jnp.int32),
            src, pos.astype(jnp.int32))


def _rope_tables(pos, reps):
    half = QK_ROPE // 2
    freq = ROPE_THETA ** (-jnp.arange(half, dtype=F32) / half)
    ang = pos.astype(F32)[:, None] * freq[None, :]
    cos, sin = jnp.cos(ang), jnp.sin(ang)
    zeros = jnp.zeros((pos.shape[0], LANES - QK_ROPE), F32)
    cos_t = jnp.concatenate([cos, cos, zeros], axis=1)
    sin_t = jnp.concatenate([-sin, sin, zeros], axis=1)
    return jnp.tile(cos_t, (reps, 1)), jnp.tile(sin_t, (reps, 1))


def kernel(x_prompt, x_sample, c_prompt, c_sample, cache_kv_latent, cache_k_rope, state_conv, page_table, w_ada, b_ada, g_pre_mix, g_post_mix, g_pre_ffn, g_post_ffn, w_in, g_q_lat, g_kv_lat, w_uq, w_uk, w_uv, conv_w, conv_b, conv_ln_g, conv_ln_b, w_out, w_router_group, b_router_group, w_router_expert, b_router_expert, w_exp_gate, w_exp_up, w_exp_down):
    n_seq_p, seq_p, d = x_prompt.shape
    n_seq_s, seq_s, _ = x_sample.shape
    depth = w_ada.shape[0]
    assert depth == 1, "single-layer trunk"
    q_lora = g_q_lat.shape[1]
    kv_lora = g_kv_lat.shape[1]
    n_heads = w_uq.shape[2]
    conv_ch = conv_w.shape[2]
    past = page_table.shape[1] * cache_kv_latent.shape[2]
    t_p, t_s = n_seq_p * seq_p, n_seq_s * seq_s
    t_all = t_p + t_s
    l = 0

    w_in_l = w_in[l]
    sp = (q_lora, q_lora + kv_lora, q_lora + kv_lora + QK_ROPE, q_lora + kv_lora + QK_ROPE + conv_ch)
    w_in_b = jnp.concatenate(
        [w_in_l[:, sp[2]:sp[3]], w_in_l[:, sp[3]:], w_in_l[:, :sp[0]], w_in_l[:, sp[0]:sp[1]],
         w_in_l[:, sp[1]:sp[2]], jnp.zeros((d, LANES - QK_ROPE), F32)], axis=1).astype(BF16)
    qa_col = (2 * conv_ch) // q_lora
    kv_col = (2 * conv_ch + q_lora) // kv_lora
    kr_col = (2 * conv_ch + q_lora + kv_lora) // LANES
    wq_b = jnp.concatenate([w_uq[l], jnp.zeros((q_lora, n_heads, HEAD_PAD - QK_HEAD), F32)],
                           axis=2).reshape(q_lora, n_heads * HEAD_PAD).astype(BF16)
    wuk_b = w_uk[l].reshape(kv_lora, n_heads * QK_NOPE).astype(BF16)
    wuv_b = w_uv[l].reshape(kv_lora, n_heads * V_HEAD).astype(BF16)
    wukT_b = jnp.transpose(w_uk[l], (1, 2, 0)).astype(BF16)
    wuv_h_b = jnp.transpose(w_uv[l], (1, 0, 2)).astype(BF16)
    w_out_b = w_out[l].astype(BF16)
    wr_b = jnp.concatenate([w_router_group[l], w_router_expert[l],
                            jnp.zeros((d, LANES - N_EXPERT_GROUPS - N_EXPERTS), F32)], axis=1).astype(BF16)
    br = jnp.concatenate([b_router_group[l], b_router_expert[l],
                          jnp.zeros((LANES - N_EXPERT_GROUPS - N_EXPERTS,), F32)]).reshape(1, LANES)

    c_rows = n_seq_p + n_seq_s
    c_pad = -c_rows % 8
    c_all = jnp.concatenate([c_prompt, c_sample, jnp.zeros((c_pad, d), F32)], axis=0)
    mod_all = _ada(c_all, w_ada[l], b_ada[l])
    mod_p = _Mod(mod_all[:n_seq_p].reshape(n_seq_p, 1, 6 * d), seq_p, d)
    mod_s = _Mod(jnp.repeat(mod_all[n_seq_p:c_rows], seq_s, axis=0).reshape(1, t_s, 6 * d), seq_s, d)

    cos_p, sin_p = _rope_tables(jnp.arange(seq_p), 1)
    cos_s, sin_s = _rope_tables(past + jnp.arange(seq_s), n_seq_s)

    xp = x_prompt.reshape(t_p, d)
    xs = x_sample.reshape(t_s, d)

    z_p = _premix(xp, g_pre_mix[l], mod_p, w_in_b)
    kv_lat_p, k_rope_p, k_full_p, v_p = _kv_post(z_p, g_kv_lat[l], cos_p, sin_p, seq_p, kv_col, kr_col, wuk_b, wuv_b)
    q_full_p = _q_proj(z_p, g_q_lat[l], cos_p, sin_p, seq_p, qa_col, wq_b)
    attn_p = _flash(q_full_p.reshape(n_seq_p, seq_p, -1), k_full_p.reshape(n_seq_p, seq_p, -1),
                    v_p.reshape(n_seq_p, seq_p, -1), n_heads).reshape(t_p, n_heads * V_HEAD)
    conv_p, st_p = _conv_prompt(z_p, n_seq_p, seq_p, conv_ch, 0, conv_w[l], conv_b[l], conv_ln_g[l], conv_ln_b[l])
    x1_p, hp_p, lg_p = _outproj(attn_p, conv_p, w_out_b, xp, mod_p, g_post_mix[l], g_pre_ffn[l], wr_b, br)

    z_s = _premix(xs, g_pre_mix[l], mod_s, w_in_b)
    kv_lat_s, k_rope_s = _kv_post(z_s, g_kv_lat[l], cos_s, sin_s, t_s, kv_col, kr_col)
    q_full_s = _q_proj(z_s, g_q_lat[l], cos_s, sin_s, t_s, qa_col, wq_b)
    q_lat_s, q_rope_hs = _absorb(q_full_s, wukT_b)
    rows = seq_s * n_heads
    q_rope_s = jnp.transpose(q_rope_hs, (1, 0, 2)).reshape(n_seq_s, rows, QK_ROPE)
    o_lat_s = _paged_attn(page_table, q_lat_s.reshape(n_seq_s, rows, kv_lora), q_rope_s,
                          kv_lat_s.reshape(n_seq_s, seq_s, kv_lora), k_rope_s.reshape(n_seq_s, seq_s, QK_ROPE),
                          cache_kv_latent, cache_k_rope, n_heads)
    attn_s = _v_up(o_lat_s.reshape(t_s, n_heads * kv_lora), wuv_h_b)
    hist = state_conv.shape[2]
    state_s = state_conv.reshape(n_seq_s, hist, conv_ch)
    conv_s_t, u_s = _conv_sample(state_s, z_s.reshape(n_seq_s, seq_s, -1), conv_ch, 0,
                                 conv_w[l], conv_b[l], conv_ln_g[l], conv_ln_b[l])
    conv_s = jnp.transpose(conv_s_t, (1, 0, 2)).reshape(t_s, conv_ch)
    x1_s, hp_s, lg_s = _outproj(attn_s, conv_s, w_out_b, xs, mod_s, g_post_mix[l], g_pre_ffn[l], wr_b, br)

    hp_all = jnp.concatenate([hp_p, hp_s], axis=0)
    ids, wts = _router(jnp.concatenate([lg_p, lg_s], axis=0))
    tm_e = 256
    nt_max = (TOP_K * t_all) // tm_e + N_EXPERTS
    tile_expert, tile_blk, tile_first, n_active, src, pos = _routing_tables(ids[:, :TOP_K], t_all, tm_e, nt_max)
    xs_sorted = _gather_rows(src, n_active * tm_e, hp_all, nt_max * tm_e)
    h_act = _moe_up(tile_expert, tile_blk, tile_first, n_active, xs_sorted, w_exp_gate, w_exp_up, tm_e)
    y_slots = _moe_down(tile_expert, tile_blk, tile_first, n_active, h_act, w_exp_down, tm_e)
    n_pairs = jnp.full((1,), TOP_K * t_all, jnp.int32)
    yg = _gather_rows(pos, n_pairs, y_slots, TOP_K * t_all)

    y_p = _final(x1_p, yg, wts, mod_p, g_post_ffn[l], 0, t_all)
    y_s = _final(x1_s, yg, wts, mod_s, g_post_ffn[l], t_p, t_all)

    conv_state_p = jnp.transpose(st_p, (0, 2, 1, 3)).reshape(n_seq_p, HALO, conv_ch)[:, HALO - (CONV_WIDTH - 1):, :]
    conv_state_s = jnp.concatenate([state_s[:, seq_s:, :], u_s], axis=1)
    return (y_p.reshape(n_seq_p, seq_p, d), y_s.reshape(n_seq_s, seq_s, d),
            kv_lat_p.reshape(1, n_seq_p, seq_p, kv_lora), k_rope_p.reshape(1, n_seq_p, seq_p, QK_ROPE),
            conv_state_p[None], kv_lat_s.reshape(1, n_seq_s, seq_s, kv_lora),
            k_rope_s.reshape(1, n_seq_s, seq_s, QK_ROPE), conv_state_s[None])
```

```python
import functools

import jax
import jax.numpy as jnp
from jax import lax
from jax.experimental import pallas as pl
from jax.experimental.pallas import tpu as pltpu

F32 = jnp.float32
BF16 = jnp.bfloat16

V_HEAD = 128
QK_NOPE = 128
QK_ROPE = 64
QK_HEAD = QK_NOPE + QK_ROPE
HEAD_PAD = 256
ROPE_THETA = 10000.0
SOFTMAX_SCALE = QK_HEAD ** -0.5
CONV_WIDTH = 31
N_EXPERT_GROUPS = 4
EXPERTS_PER_GROUP = 8
N_EXPERTS = N_EXPERT_GROUPS * EXPERTS_PER_GROUP
TOP_K = 2
NORM_EPS = 1e-6
LN_EPS = 1e-5
LANES = 128
HALO = 32

MIB = 1024 * 1024


def _cparams(vmem_mib, n_axes):
    return pltpu.CompilerParams(dimension_semantics=("arbitrary",) * n_axes,
                                vmem_limit_bytes=vmem_mib * MIB)


def _sigmoid(x):
    return 1.0 / (1.0 + jnp.exp(-x))


def _rms(x):
    return x * lax.rsqrt(jnp.mean(x * x, axis=-1, keepdims=True) + NORM_EPS)


def _rope_chunk(c, cos, sin):
    lane = lax.broadcasted_iota(jnp.int32, c.shape, 1)
    sw = jnp.where(lane < QK_ROPE // 2, pltpu.roll(c, LANES - QK_ROPE // 2, 1), pltpu.roll(c, QK_ROPE // 2, 1))
    return c * cos + sw * sin


def _ada_kernel(c_ref, w_ref, b_ref, o_ref):
    c = c_ref[...]
    a = (c * _sigmoid(c)).astype(BF16)
    o_ref[...] = jnp.dot(a, w_ref[...].astype(BF16), preferred_element_type=F32) + b_ref[...]


def _ada(c_all, w_ada, b_ada):
    m, d = c_all.shape
    n = w_ada.shape[1]
    tn = 512
    return pl.pallas_call(
        _ada_kernel,
        grid=(n // tn,),
        in_specs=[pl.BlockSpec((m, d), lambda j: (0, 0)),
                  pl.BlockSpec((d, tn), lambda j: (0, j)),
                  pl.BlockSpec((1, tn), lambda j: (0, j))],
        out_specs=pl.BlockSpec((m, tn), lambda j: (0, j)),
        out_shape=jax.ShapeDtypeStruct((m, n), F32),
        compiler_params=_cparams(40, 1),
        name="ada_ln",
    )(c_all, w_ada, b_ada.reshape(1, n))


class _Mod:
    def __init__(self, arr, rows_per_seq, d):
        self.arr = arr
        self.rows_per_seq = rows_per_seq
        self.d = d
        self.per_row = arr.shape[1] != 1

    def spec(self, tm, k, n_grid_axes):
        d = self.d
        if self.per_row:
            if n_grid_axes == 1:
                return pl.BlockSpec((1, tm, d), lambda i: (0, i, k))
            return pl.BlockSpec((1, tm, d), lambda i, j: (0, i, k))
        tiles = self.rows_per_seq // tm
        if n_grid_axes == 1:
            return pl.BlockSpec((1, 1, d), lambda i: (i // tiles, 0, k))
        return pl.BlockSpec((1, 1, d), lambda i, j: (i // tiles, 0, k))


def _premix_kernel(x_ref, g_ref, sc_ref, sh_ref, w_ref, o_ref, h_ref):
    @pl.when(pl.program_id(1) == 0)
    def _():
        h = _rms(x_ref[...]) * g_ref[...]
        h = h * (1.0 + sc_ref[0]) + sh_ref[0]
        h_ref[...] = h.astype(BF16)

    o_ref[...] = jnp.dot(h_ref[...], w_ref[...], preferred_element_type=F32)


def _premix(x, g, mod, w_b):
    t, d = x.shape
    n = w_b.shape[1]
    tm, tn = (128 if mod.per_row else 512), 640
    return pl.pallas_call(
        _premix_kernel,
        grid=(t // tm, n // tn),
        in_specs=[pl.BlockSpec((tm, d), lambda i, j: (i, 0)),
                  pl.BlockSpec((1, d), lambda i, j: (0, 0)),
                  mod.spec(tm, 1, 2), mod.spec(tm, 0, 2),
                  pl.BlockSpec((d, tn), lambda i, j: (0, j))],
        out_specs=pl.BlockSpec((tm, tn), lambda i, j: (i, j)),
        out_shape=jax.ShapeDtypeStruct((t, n), F32),
        scratch_shapes=[pltpu.VMEM((tm, d), BF16)],
        compiler_params=_cparams(48, 2),
        name="premix_in_proj",
    )(x, g.reshape(1, d), mod.arr, mod.arr, w_b)


def _kv_kernel(with_up, n_heads, kva_ref, kr_ref, g_ref, cos_ref, sin_ref, *rest):
    if with_up:
        wuk_ref, wuv_ref, kv_out, kr_out, kfull_out, v_out = rest
    else:
        kv_out, kr_out = rest
    kv = _rms(kva_ref[...]) * g_ref[...]
    kv_out[...] = kv
    rot = _rope_chunk(kr_ref[...], cos_ref[...], sin_ref[...])
    kr_out[...] = rot[:, :QK_ROPE]
    if with_up:
        kvb = kv.astype(BF16)
        k_nope = jnp.dot(kvb, wuk_ref[...], preferred_element_type=F32)
        v_out[...] = jnp.dot(kvb, wuv_ref[...], preferred_element_type=F32).astype(BF16)
        rot_b = rot.astype(BF16)
        for h in range(n_heads):
            kfull_out[:, h * HEAD_PAD:h * HEAD_PAD + QK_NOPE] = k_nope[:, h * QK_NOPE:(h + 1) * QK_NOPE].astype(BF16)
            kfull_out[:, h * HEAD_PAD + QK_NOPE:(h + 1) * HEAD_PAD] = rot_b


def _kv_post(z, g_kv, cos_t, sin_t, pos_rows, kv_col, kr_col, wuk_b=None, wuv_b=None):
    t = z.shape[0]
    kv_lora = g_kv.shape[0]
    tm = 256
    pos_tiles = pos_rows // tm
    with_up = wuk_b is not None
    in_specs = [pl.BlockSpec((tm, kv_lora), lambda i: (i, kv_col)),
                pl.BlockSpec((tm, LANES), lambda i: (i, kr_col)),
                pl.BlockSpec((1, kv_lora), lambda i: (0, 0)),
                pl.BlockSpec((tm, LANES), lambda i: (i % pos_tiles, 0)),
                pl.BlockSpec((tm, LANES), lambda i: (i % pos_tiles, 0))]
    args = [z, z, g_kv.reshape(1, kv_lora), cos_t, sin_t]
    out_specs = [pl.BlockSpec((tm, kv_lora), lambda i: (i, 0)),
                 pl.BlockSpec((tm, QK_ROPE), lambda i: (i, 0))]
    out_shape = [jax.ShapeDtypeStruct((t, kv_lora), F32), jax.ShapeDtypeStruct((t, QK_ROPE), F32)]
    n_heads = 0
    if with_up:
        n_heads = wuk_b.shape[1] // QK_NOPE
        in_specs += [pl.BlockSpec(wuk_b.shape, lambda i: (0, 0)), pl.BlockSpec(wuv_b.shape, lambda i: (0, 0))]
        args += [wuk_b, wuv_b]
        out_specs += [pl.BlockSpec((tm, n_heads * HEAD_PAD), lambda i: (i, 0)),
                      pl.BlockSpec((tm, n_heads * V_HEAD), lambda i: (i, 0))]
        out_shape += [jax.ShapeDtypeStruct((t, n_heads * HEAD_PAD), BF16),
                      jax.ShapeDtypeStruct((t, n_heads * V_HEAD), BF16)]
    return pl.pallas_call(
        functools.partial(_kv_kernel, with_up, n_heads),
        grid=(t // tm,),
        in_specs=in_specs, out_specs=out_specs, out_shape=out_shape,
        compiler_params=_cparams(32, 1),
        name="kv_post_up" if with_up else "kv_post",
    )(*args)


def _q_kernel(heads_per_step, qa_ref, g_ref, cos_ref, sin_ref, w_ref, o_ref):
    qn = (_rms(qa_ref[...]) * g_ref[...]).astype(BF16)
    q = jnp.dot(qn, w_ref[...], preferred_element_type=F32)
    cos = cos_ref[...]
    sin = sin_ref[...]
    for h in range(heads_per_step):
        lo = h * HEAD_PAD
        o_ref[:, lo:lo + QK_NOPE] = q[:, lo:lo + QK_NOPE].astype(BF16)
        o_ref[:, lo + QK_NOPE:lo + HEAD_PAD] = _rope_chunk(q[:, lo + QK_NOPE:lo + HEAD_PAD], cos, sin).astype(BF16)


def _q_proj(z, g_q, cos_t, sin_t, pos_rows, qa_col, wq_b):
    t = z.shape[0]
    q_lora, n = wq_b.shape
    tm, tn = 256, 1024
    pos_tiles = pos_rows // tm
    return pl.pallas_call(
        functools.partial(_q_kernel, tn // HEAD_PAD),
        grid=(t // tm, n // tn),
        in_specs=[pl.BlockSpec((tm, q_lora), lambda i, j: (i, qa_col)),
                  pl.BlockSpec((1, q_lora), lambda i, j: (0, 0)),
                  pl.BlockSpec((tm, LANES), lambda i, j: (i % pos_tiles, 0)),
                  pl.BlockSpec((tm, LANES), lambda i, j: (i % pos_tiles, 0)),
                  pl.BlockSpec((q_lora, tn), lambda i, j: (0, j))],
        out_specs=pl.BlockSpec((tm, tn), lambda i, j: (i, j)),
        out_shape=jax.ShapeDtypeStruct((t, n), BF16),
        compiler_params=_cparams(32, 2),
        name="q_proj_rope",
    )(z, g_q.reshape(1, q_lora), cos_t, sin_t, wq_b)


def _flash_kernel(tb, g_heads, it_ref, jt_ref, q_ref, k_ref, v_ref, o_ref, m_ref, l_ref, acc_ref):
    pair = pl.program_id(2)
    i = it_ref[pair]
    j = jt_ref[pair]

    @pl.when(j == 0)
    def _():
        m_ref[...] = jnp.full(m_ref.shape, -jnp.inf, F32)
        l_ref[...] = jnp.zeros(l_ref.shape, F32)
        acc_ref[...] = jnp.zeros(acc_ref.shape, F32)

    def step(diagonal):
        for g in range(g_heads):
            q = q_ref[:, g * HEAD_PAD:(g + 1) * HEAD_PAD]
            k = k_ref[:, g * HEAD_PAD:(g + 1) * HEAD_PAD]
            s = lax.dot_general(q, k, (((1,), (1,)), ((), ())), preferred_element_type=F32) * SOFTMAX_SCALE
            if diagonal:
                row = lax.broadcasted_iota(jnp.int32, s.shape, 0)
                col = lax.broadcasted_iota(jnp.int32, s.shape, 1)
                s = jnp.where(col <= row, s, -jnp.inf)
            m_prev = m_ref[g]
            m_new = jnp.maximum(m_prev, jnp.max(s, axis=-1, keepdims=True))
            alpha = jnp.exp(m_prev - m_new)
            p = jnp.exp(s - m_new)
            l_ref[g] = alpha * l_ref[g] + jnp.sum(p, axis=-1, keepdims=True)
            acc_ref[g] = alpha * acc_ref[g] + jnp.dot(p.astype(BF16), v_ref[:, g * V_HEAD:(g + 1) * V_HEAD],
                                                      preferred_element_type=F32)
            m_ref[g] = m_new

    @pl.when(j < i)
    def _():
        step(False)

    @pl.when(j == i)
    def _():
        step(True)
        for g in range(g_heads):
            o_ref[:, g * V_HEAD:(g + 1) * V_HEAD] = (acc_ref[g] / l_ref[g]).astype(o_ref.dtype)


def _flash(q_full, k_full, v, n_heads):
    b, t, _ = q_full.shape
    tb = 512
    g_heads = 2
    nb = t // tb
    pairs = [(i, j) for i in range(nb) for j in range(i + 1)]
    it = jnp.asarray([p[0] for p in pairs], jnp.int32)
    jt = jnp.asarray([p[1] for p in pairs], jnp.int32)
    grid_spec = pltpu.PrefetchScalarGridSpec(
        num_scalar_prefetch=2,
        grid=(b, n_heads // g_heads, len(pairs)),
        in_specs=[pl.BlockSpec((None, tb, g_heads * HEAD_PAD), lambda bb, h, p, it_r, jt_r: (bb, it_r[p], h)),
                  pl.BlockSpec((None, tb, g_heads * HEAD_PAD), lambda bb, h, p, it_r, jt_r: (bb, jt_r[p], h)),
                  pl.BlockSpec((None, tb, g_heads * V_HEAD), lambda bb, h, p, it_r, jt_r: (bb, jt_r[p], h))],
        out_specs=pl.BlockSpec((None, tb, g_heads * V_HEAD), lambda bb, h, p, it_r, jt_r: (bb, it_r[p], h)),
        scratch_shapes=[pltpu.VMEM((g_heads, tb, 1), F32), pltpu.VMEM((g_heads, tb, 1), F32),
                        pltpu.VMEM((g_heads, tb, V_HEAD), F32)])
    return pl.pallas_call(
        functools.partial(_flash_kernel, tb, g_heads),
        grid_spec=grid_spec,
        out_shape=jax.ShapeDtypeStruct((b, t, n_heads * V_HEAD), BF16),
        compiler_params=_cparams(32, 3),
        name="prompt_flash_attn",
    )(it, jt, q_full, k_full, v)


def _absorb_kernel(q_ref, w_ref, ql_ref, qr_ref):
    q = q_ref[...]
    ql_ref[...] = jnp.dot(q[:, :QK_NOPE], w_ref[...], preferred_element_type=F32).astype(BF16)
    qr_ref[...] = q[:, QK_NOPE:QK_NOPE + QK_ROPE]


def _absorb(q_full, wukT_b):
    t = q_full.shape[0]
    n_heads, _, kv_lora = wukT_b.shape
    return pl.pallas_call(
        _absorb_kernel,
        grid=(n_heads,),
        in_specs=[pl.BlockSpec((t, HEAD_PAD), lambda h: (0, h)),
                  pl.BlockSpec((None, QK_NOPE, kv_lora), lambda h: (h, 0, 0))],
        out_specs=[pl.BlockSpec((t, kv_lora), lambda h: (0, h)),
                   pl.BlockSpec((None, t, QK_ROPE), lambda h: (h, 0, 0))],
        out_shape=[jax.ShapeDtypeStruct((t, n_heads * kv_lora), BF16),
                   jax.ShapeDtypeStruct((n_heads, t, QK_ROPE), BF16)],
        compiler_params=_cparams(32, 1),
        name="sample_absorb_q",
    )(q_full, wukT_b)


def _paged_kernel(pages, page_size, n_chunks, n_heads, pt_ref, ql_ref, qr_ref, kvn_ref, krn_ref, *rest):
    kv_refs = rest[:pages]
    kr_refs = rest[pages:2 * pages]
    o_ref, kbuf, rbuf, m_ref, l_ref, acc_ref = rest[2 * pages:]
    c = pl.program_id(1)

    @pl.when(c == 0)
    def _():
        m_ref[...] = jnp.full(m_ref.shape, -jnp.inf, F32)
        l_ref[...] = jnp.zeros(l_ref.shape, F32)
        acc_ref[...] = jnp.zeros(acc_ref.shape, F32)

    for p in range(pages):
        kbuf[p * page_size:(p + 1) * page_size, :] = kv_refs[p][...].astype(BF16)
        rbuf[:, p * page_size:(p + 1) * page_size] = kr_refs[p][...].astype(BF16)

    ql = ql_ref[...]
    qr = qr_ref[...]
    nt = (((1,), (1,)), ((), ()))
    s = (lax.dot_general(ql, kbuf[...], nt, preferred_element_type=F32)
         + jnp.dot(qr, rbuf[...], preferred_element_type=F32)) * SOFTMAX_SCALE
    m_prev = m_ref[...]
    m_new = jnp.maximum(m_prev, jnp.max(s, axis=-1, keepdims=True))
    alpha = jnp.exp(m_prev - m_new)
    p_ = jnp.exp(s - m_new)
    l_ref[...] = alpha * l_ref[...] + jnp.sum(p_, axis=-1, keepdims=True)
    acc_ref[...] = alpha * acc_ref[...] + jnp.dot(p_.astype(BF16), kbuf[...], preferred_element_type=F32)
    m_ref[...] = m_new

    @pl.when(c == n_chunks - 1)
    def _():
        qlf = ql.astype(F32)
        qrf = qr.astype(F32)
        kvn = kvn_ref[...].astype(BF16).astype(F32)
        krn = krn_ref[...].astype(BF16).astype(F32)
        n_new = kvn.shape[0]
        tok = lax.broadcasted_iota(jnp.int32, (ql.shape[0], 1), 0) // n_heads
        s_new = []
        for jn in range(n_new):
            sj = (jnp.sum(qlf * kvn[jn:jn + 1, :], axis=-1, keepdims=True)
                  + jnp.sum(qrf * krn[jn:jn + 1, :], axis=-1, keepdims=True)) * SOFTMAX_SCALE
            s_new.append(jnp.where(tok >= jn, sj, -jnp.inf))
        m_prev2 = m_ref[...]
        m_fin = m_prev2
        for sj in s_new:
            m_fin = jnp.maximum(m_fin, sj)
        alpha2 = jnp.exp(m_prev2 - m_fin)
        l_fin = alpha2 * l_ref[...]
        acc = alpha2 * acc_ref[...]
        for jn, sj in enumerate(s_new):
            pj = jnp.exp(sj - m_fin)
            l_fin = l_fin + pj
            acc = acc + pj.astype(BF16).astype(F32) * kvn[jn:jn + 1, :]
        o_ref[...] = (acc / l_fin).astype(o_ref.dtype)


def _paged_attn(page_table, q_lat, q_rope, kv_new, kr_new, cache_kv, cache_kr_t, n_heads):
    n_seq, rows, kv_lora = q_lat.shape
    page_size = cache_kv.shape[2]
    n_pages = page_table.shape[1]
    n_new = kv_new.shape[1]
    pages = 32
    n_chunks = n_pages // pages

    def page_spec(p, shape):
        return pl.BlockSpec((None, None) + shape, lambda b, c, pt: (0, pt[b, c * pages + p], 0, 0))

    in_specs = [pl.BlockSpec((None, rows, kv_lora), lambda b, c, pt: (b, 0, 0)),
                pl.BlockSpec((None, rows, QK_ROPE), lambda b, c, pt: (b, 0, 0)),
                pl.BlockSpec((None, n_new, kv_lora), lambda b, c, pt: (b, 0, 0)),
                pl.BlockSpec((None, n_new, QK_ROPE), lambda b, c, pt: (b, 0, 0))]
    in_specs += [page_spec(p, (page_size, kv_lora)) for p in range(pages)]
    in_specs += [page_spec(p, (QK_ROPE, page_size)) for p in range(pages)]
    grid_spec = pltpu.PrefetchScalarGridSpec(
        num_scalar_prefetch=1,
        grid=(n_seq, n_chunks),
        in_specs=in_specs,
        out_specs=pl.BlockSpec((None, rows, kv_lora), lambda b, c, pt: (b, 0, 0)),
        scratch_shapes=[pltpu.VMEM((pages * page_size, kv_lora), BF16),
                        pltpu.VMEM((QK_ROPE, pages * page_size), BF16),
                        pltpu.VMEM((rows, 1), F32), pltpu.VMEM((rows, 1), F32),
                        pltpu.VMEM((rows, kv_lora), F32)])
    return pl.pallas_call(
        functools.partial(_paged_kernel, pages, page_size, n_chunks, n_heads),
        grid_spec=grid_spec,
        out_shape=jax.ShapeDtypeStruct((n_seq, rows, kv_lora), BF16),
        compiler_params=_cparams(40, 2),
        name="sample_paged_attn",
    )(page_table, q_lat, q_rope, kv_new, kr_new, *([cache_kv] * pages), *([cache_kr_t] * pages))


def _vup_kernel(o_ref, w_ref, a_ref):
    a_ref[...] = jnp.dot(o_ref[...], w_ref[...], preferred_element_type=F32).astype(BF16)


def _v_up(o_lat, wuv_h_b):
    t = o_lat.shape[0]
    n_heads, kv_lora, _ = wuv_h_b.shape
    return pl.pallas_call(
        _vup_kernel,
        grid=(n_heads,),
        in_specs=[pl.BlockSpec((t, kv_lora), lambda h: (0, h)),
                  pl.BlockSpec((None, kv_lora, V_HEAD), lambda h: (h, 0, 0))],
        out_specs=pl.BlockSpec((t, V_HEAD), lambda h: (0, h)),
        out_shape=jax.ShapeDtypeStruct((t, n_heads * V_HEAD), BF16),
        compiler_params=_cparams(32, 1),
        name="sample_v_up",
    )(o_lat, wuv_h_b)


def _ln_swish(y, g, b):
    mu = jnp.mean(y, axis=-1, keepdims=True)
    yc = y - mu
    var = jnp.mean(yc * yc, axis=-1, keepdims=True)
    o = yc * lax.rsqrt(var + LN_EPS) * g + b
    return o * _sigmoid(o)


def _conv_prompt_kernel(tt, cw, n_cc, val_ref, gate_ref, hval_ref, hgate_ref, w_ref, b_ref, g_ref, lb_ref,
                        o_ref, st_ref, ext_ref, y_ref):
    i = pl.program_id(1)
    cc = pl.program_id(2)
    um = val_ref[...] * _sigmoid(gate_ref[...])
    uh = hval_ref[...] * _sigmoid(hgate_ref[...])
    uh = jnp.where(i == 0, 0.0, uh)
    ext_ref[0:HALO, :] = uh
    ext_ref[HALO:HALO + tt, :] = um
    st_ref[cc] = um[tt - HALO:, :]
    rb = 64
    w = w_ref[...]
    bias = b_ref[...]
    for r in range(0, tt, rb):
        acc = jnp.zeros((rb, cw), F32) + bias
        for k in range(CONV_WIDTH):
            s0 = r + k + HALO - (CONV_WIDTH - 1)
            acc = acc + w[k:k + 1, :] * ext_ref[s0:s0 + rb, :]
        y_ref[cc, r:r + rb, :] = acc

    @pl.when(cc == n_cc - 1)
    def _():
        s1 = jnp.zeros((tt, 1), F32)
        for c2 in range(n_cc):
            s1 = s1 + jnp.sum(y_ref[c2], axis=-1, keepdims=True)
        mu = s1 / (n_cc * cw)
        s2 = jnp.zeros((tt, 1), F32)
        for c2 in range(n_cc):
            d = y_ref[c2] - mu
            s2 = s2 + jnp.sum(d * d, axis=-1, keepdims=True)
        rs = lax.rsqrt(s2 / (n_cc * cw) + LN_EPS)
        for c2 in range(n_cc):
            o = (y_ref[c2] - mu) * rs * g_ref[:, c2 * cw:(c2 + 1) * cw] + lb_ref[:, c2 * cw:(c2 + 1) * cw]
            o_ref[:, c2 * cw:(c2 + 1) * cw] = (o * _sigmoid(o)).astype(o_ref.dtype)


def _conv_prompt(z, n_seq, seq, conv_ch, val_col0, conv_w, conv_b, ln_g, ln_b):
    tt, cw = 256, 256
    n_cc = conv_ch // cw
    nt = seq // tt
    hb = tt // HALO
    gate_off = conv_ch // cw

    def main(off):
        return pl.BlockSpec((tt, cw), lambda b, i, c: (b * nt + i, val_col0 + off + c))

    def halo(off):
        return pl.BlockSpec((HALO, cw), lambda b, i, c: (jnp.maximum((b * nt + i) * hb - 1, 0), val_col0 + off + c))

    return pl.pallas_call(
        functools.partial(_conv_prompt_kernel, tt, cw, n_cc),
        grid=(n_seq, nt, n_cc),
        in_specs=[main(0), main(gate_off), halo(0), halo(gate_off),
                  pl.BlockSpec((CONV_WIDTH, cw), lambda b, i, c: (0, c)),
                  pl.BlockSpec((1, cw), lambda b, i, c: (0, c)),
                  pl.BlockSpec((1, conv_ch), lambda b, i, c: (0, 0)),
                  pl.BlockSpec((1, conv_ch), lambda b, i, c: (0, 0))],
        out_specs=[pl.BlockSpec((tt, conv_ch), lambda b, i, c: (b * nt + i, 0)),
                   pl.BlockSpec((None, n_cc, HALO, cw), lambda b, i, c: (b, 0, 0, 0))],
        out_shape=[jax.ShapeDtypeStruct((n_seq * seq, conv_ch), BF16),
                   jax.ShapeDtypeStruct((n_seq, n_cc, HALO, cw), F32)],
        scratch_shapes=[pltpu.VMEM((HALO + tt, cw), F32), pltpu.VMEM((n_cc, tt, cw), F32)],
        compiler_params=_cparams(32, 3),
        name="conv_prompt",
    )(z, z, z, z, conv_w, conv_b.reshape(1, conv_ch), ln_g.reshape(1, conv_ch), ln_b.reshape(1, conv_ch))


def _conv_sample_kernel(n_new, st_ref, val_ref, gate_ref, wst_ref, wu_ref, b_ref, g_ref, lb_ref, o_ref, u_ref):
    u = val_ref[...] * _sigmoid(gate_ref[...])
    u_ref[...] = u
    st = st_ref[...]
    for t in range(n_new):
        y = (jnp.sum(st * wst_ref[t][None], axis=1) + jnp.sum(u * wu_ref[t][None], axis=1) + b_ref[...])
        o_ref[t] = _ln_swish(y, g_ref[...], lb_ref[...]).astype(o_ref.dtype)


def _conv_sample(state, z3, conv_ch, val_col0, conv_w, conv_b, ln_g, ln_b):
    n_seq, hist, _ = state.shape
    n_new = z3.shape[1]
    nb = 16
    jj = jnp.arange(hist)[None, :] - jnp.arange(n_new)[:, None]
    wst = jnp.where((jj >= 0)[..., None], conv_w[jnp.clip(jj, 0, CONV_WIDTH - 1)], 0.0)
    ii = hist - jnp.arange(n_new)[:, None] + jnp.arange(n_new)[None, :]
    wu = jnp.where((ii <= CONV_WIDTH - 1)[..., None], conv_w[jnp.clip(ii, 0, CONV_WIDTH - 1)], 0.0)
    gate_off = 1
    return pl.pallas_call(
        functools.partial(_conv_sample_kernel, n_new),
        grid=(n_seq // nb,),
        in_specs=[pl.BlockSpec((nb, hist, conv_ch), lambda i: (i, 0, 0)),
                  pl.BlockSpec((nb, n_new, conv_ch), lambda i: (i, 0, val_col0)),
                  pl.BlockSpec((nb, n_new, conv_ch), lambda i: (i, 0, val_col0 + gate_off)),
                  pl.BlockSpec((n_new, hist, conv_ch), lambda i: (0, 0, 0)),
                  pl.BlockSpec((n_new, n_new, conv_ch), lambda i: (0, 0, 0)),
                  pl.BlockSpec((1, conv_ch), lambda i: (0, 0)),
                  pl.BlockSpec((1, conv_ch), lambda i: (0, 0)),
                  pl.BlockSpec((1, conv_ch), lambda i: (0, 0))],
        out_specs=[pl.BlockSpec((n_new, nb, conv_ch), lambda i: (0, i, 0)),
                   pl.BlockSpec((nb, n_new, conv_ch), lambda i: (i, 0, 0))],
        out_shape=[jax.ShapeDtypeStruct((n_new, n_seq, conv_ch), BF16),
                   jax.ShapeDtypeStruct((n_seq, n_new, conv_ch), F32)],
        compiler_params=_cparams(32, 1),
        name="conv_sample",
    )(state, z3, z3, wst, wu, conv_b.reshape(1, conv_ch), ln_g.reshape(1, conv_ch), ln_b.reshape(1, conv_ch))


def _pack_pair(lo, hi):
    lo_u = lax.bitcast_convert_type(lo.astype(BF16).astype(F32), jnp.uint32)
    hi_u = lax.bitcast_convert_type(hi.astype(BF16).astype(F32), jnp.uint32)
    return (lo_u >> 16) | (hi_u & jnp.uint32(0xFFFF0000))


def _unpack_pair(w):
    lo = lax.bitcast_convert_type(w << 16, F32).astype(BF16)
    hi = lax.bitcast_convert_type(w & jnp.uint32(0xFFFF0000), F32).astype(BF16)
    return lo, hi


def _outproj_kernel(nj, tn, a_ref, c_ref, wt_ref, wb_ref, x_ref, ga_ref, gpost_ref, gpre_ref, sc_ref, sh_ref,
                    wr_ref, br_ref, x1_ref, hp_ref, lg_ref, mix_ref):
    j = pl.program_id(1)
    mix_ref[j] = (jnp.dot(a_ref[...], wt_ref[...], preferred_element_type=F32)
                  + jnp.dot(c_ref[...], wb_ref[...], preferred_element_type=F32))

    @pl.when(j == nj - 1)
    def _():
        tm = x_ref.shape[0]
        d = nj * tn
        ss = jnp.zeros((tm, 1), F32)
        for c in range(nj):
            m = mix_ref[c]
            ss = ss + jnp.sum(m * m, axis=-1, keepdims=True)
        r = lax.rsqrt(ss / d + NORM_EPS)
        ss1 = jnp.zeros((tm, 1), F32)
        for c in range(nj):
            cs = slice(c * tn, (c + 1) * tn)
            x1 = x_ref[:, cs] + ga_ref[0, :, cs] * (mix_ref[c] * r * gpost_ref[:, cs])
            x1_ref[:, cs] = x1
            ss1 = ss1 + jnp.sum(x1 * x1, axis=-1, keepdims=True)
        r1 = lax.rsqrt(ss1 / d + NORM_EPS)
        half = nj // 2
        lg = jnp.zeros(lg_ref.shape, F32) + br_ref[...]
        for c in range(half):
            parts = []
            for cc in (c, c + half):
                cs = slice(cc * tn, (cc + 1) * tn)
                h = (x1_ref[:, cs] * r1 * gpre_ref[:, cs]) * (1.0 + sc_ref[0, :, cs]) + sh_ref[0, :, cs]
                lg = lg + jnp.dot(h.astype(BF16), wr_ref[cs, :], preferred_element_type=F32)
                parts.append(h)
            packed = _pack_pair(parts[0], parts[1])
            ks = d // 2 // LANES
            for q in range(tn // LANES):
                k = c * (tn // LANES) + q
                hp_ref[pl.ds(k, tm, stride=ks), :] = packed[:, q * LANES:(q + 1) * LANES]
        lg_ref[...] = lg


def _outproj(attn, conv, w_out_b, x, mod, g_post, g_pre, wr_b, br):
    t, d = x.shape
    kh = attn.shape[1]
    tm, tn = (128 if mod.per_row else 256), 512
    nj = d // tn
    ks = d // 2 // LANES
    return pl.pallas_call(
        functools.partial(_outproj_kernel, nj, tn),
        grid=(t // tm, nj),
        in_specs=[pl.BlockSpec((tm, kh), lambda i, j: (i, 0)),
                  pl.BlockSpec((tm, kh), lambda i, j: (i, 0)),
                  pl.BlockSpec((kh, tn), lambda i, j: (0, j)),
                  pl.BlockSpec((kh, tn), lambda i, j: (1, j)),
                  pl.BlockSpec((tm, d), lambda i, j: (i, 0)),
                  mod.spec(tm, 2, 2),
                  pl.BlockSpec((1, d), lambda i, j: (0, 0)),
                  pl.BlockSpec((1, d), lambda i, j: (0, 0)),
                  mod.spec(tm, 4, 2), mod.spec(tm, 3, 2),
                  pl.BlockSpec((d, LANES), lambda i, j: (0, 0)),
                  pl.BlockSpec((1, LANES), lambda i, j: (0, 0))],
        out_specs=[pl.BlockSpec((tm, d), lambda i, j: (i, 0)),
                   pl.BlockSpec((tm * ks, LANES), lambda i, j: (i, 0)),
                   pl.BlockSpec((tm, LANES), lambda i, j: (i, 0))],
        out_shape=[jax.ShapeDtypeStruct((t, d), F32),
                   jax.ShapeDtypeStruct((t * ks, LANES), jnp.uint32),
                   jax.ShapeDtypeStruct((t, LANES), F32)],
        scratch_shapes=[pltpu.VMEM((nj, tm, tn), F32)],
        compiler_params=_cparams(48, 2),
        name="out_proj_residual_prenorm",
    )(attn, conv, w_out_b, w_out_b, x, mod.arr, g_post.reshape(1, d), g_pre.reshape(1, d), mod.arr, mod.arr,
      wr_b, br)


def _router_kernel(lg_ref, id_ref, wt_ref):
    lg = lg_ref[...]
    lane = lax.broadcasted_iota(jnp.int32, lg.shape, 1)
    is_g = lane < N_EXPERT_GROUPS
    gmax = jnp.max(jnp.where(is_g, lg, -jnp.inf), axis=-1, keepdims=True)
    gexp = jnp.where(is_g, jnp.exp(jnp.where(is_g, lg, gmax) - gmax), 0.0)
    gprob = gexp / jnp.sum(gexp, axis=-1, keepdims=True)
    gw = jnp.max(gprob, axis=-1, keepdims=True)
    gidx = jnp.min(jnp.where(is_g & (gprob == gw), lane, LANES), axis=-1, keepdims=True)
    lo = N_EXPERT_GROUPS + gidx * EXPERTS_PER_GROUP
    is_e = (lane >= lo) & (lane < lo + EXPERTS_PER_GROUP)
    emax = jnp.max(jnp.where(is_e, lg, -jnp.inf), axis=-1, keepdims=True)
    eexp = jnp.where(is_e, jnp.exp(jnp.where(is_e, lg, emax) - emax), 0.0)
    ep = jnp.where(is_e, eexp / jnp.sum(eexp, axis=-1, keepdims=True), -1.0)
    p1 = jnp.max(ep, axis=-1, keepdims=True)
    i1 = jnp.min(jnp.where(ep == p1, lane, LANES), axis=-1, keepdims=True)
    ep2 = jnp.where(lane == i1, -1.0, ep)
    p2 = jnp.max(ep2, axis=-1, keepdims=True)
    i2 = jnp.min(jnp.where(ep2 == p2, lane, LANES), axis=-1, keepdims=True)
    den = p1 + p2
    w1 = p1 / den * gw
    w2 = p2 / den * gw
    id_ref[...] = jnp.where(lane == 0, i1 - N_EXPERT_GROUPS, jnp.where(lane == 1, i2 - N_EXPERT_GROUPS, 0))
    wt_ref[...] = jnp.where(lane == 0, w1, jnp.where(lane == 1, w2, 0.0))


def _router(logits):
    t = logits.shape[0]
    tm = 512
    spec = pl.BlockSpec((tm, LANES), lambda i: (i, 0))
    return pl.pallas_call(
        _router_kernel,
        grid=(t // tm,),
        in_specs=[spec], out_specs=[spec, spec],
        out_shape=[jax.ShapeDtypeStruct((t, LANES), jnp.int32), jax.ShapeDtypeStruct((t, LANES), F32)],
        compiler_params=_cparams(32, 1),
        name="router_topk",
    )(logits)


def _token_gather_kernel(tm, ks, src_ref, tb_ref, na_ref, hp_ref, o_ref, buf, sem):
    t = pl.program_id(0)
    na = na_ref[0]

    def issue(tile, slot):
        def body(r, carry):
            tok = src_ref[tile * tm + r]
            pltpu.make_async_copy(hp_ref.at[pl.ds(pl.multiple_of(tok * ks, ks), ks)],
                                  buf.at[slot, pl.ds(pl.multiple_of(r * ks, ks), ks)],
                                  sem.at[slot]).start()
            return carry
        lax.fori_loop(0, tm, body, 0)

    @pl.when(t == 0)
    def _():
        issue(0, 0)

    @pl.when(t + 1 < na)
    def _():
        issue(t + 1, (t + 1) % 2)

    @pl.when(t < na)
    def _():
        slot = t % 2
        pltpu.make_async_copy(hp_ref.at[pl.ds(0, tm * ks)], buf.at[slot], sem.at[slot]).wait()
        kh = ks * LANES
        for k in range(ks):
            lo, hi = _unpack_pair(buf[slot, pl.ds(k, tm, stride=ks), :])
            o_ref[:, k * LANES:(k + 1) * LANES] = lo
            o_ref[:, kh + k * LANES:kh + (k + 1) * LANES] = hi


def _token_gather(src, tile_blk, n_active, hp, tm, d):
    s = src.shape[0]
    ks = d // 2 // LANES
    grid_spec = pltpu.PrefetchScalarGridSpec(
        num_scalar_prefetch=3,
        grid=(s // tm,),
        in_specs=[pl.BlockSpec(memory_space=pl.ANY)],
        out_specs=pl.BlockSpec((tm, d), lambda t, src_r, tb, na: (tb[t], 0)),
        scratch_shapes=[pltpu.VMEM((2, tm * ks, LANES), jnp.uint32), pltpu.SemaphoreType.DMA((2,))])
    return pl.pallas_call(
        functools.partial(_token_gather_kernel, tm, ks),
        grid_spec=grid_spec,
        out_shape=jax.ShapeDtypeStruct((s, d), BF16),
        compiler_params=_cparams(32, 1),
        name="token_gather",
    )(src, tile_blk, n_active, hp)


def _moe_up_kernel(te_ref, tb_ref, tf_ref, na_ref, xs_ref, wg_ref, wu_ref, h_ref, wg_b, wu_b):
    t = pl.program_id(1)

    @pl.when(tf_ref[t] == 1)
    def _():
        wg_b[...] = wg_ref[...].astype(BF16)
        wu_b[...] = wu_ref[...].astype(BF16)

    @pl.when(t < na_ref[0])
    def _():
        x = xs_ref[...]
        g = jnp.dot(x, wg_b[...], preferred_element_type=F32)
        u = jnp.dot(x, wu_b[...], preferred_element_type=F32)
        h_ref[...] = (g * _sigmoid(g) * u).astype(h_ref.dtype)


def _moe_up(tile_expert, tile_blk, tile_first, n_active, xs, w_gate, w_up, tm):
    s, d = xs.shape
    f = w_gate.shape[3]
    tf = 512
    nt = s // tm
    grid_spec = pltpu.PrefetchScalarGridSpec(
        num_scalar_prefetch=4,
        grid=(f // tf, nt),
        in_specs=[pl.BlockSpec((tm, d), lambda c, t, te, tb, tfi, na: (tb[t], 0)),
                  pl.BlockSpec((None, None, d, tf), lambda c, t, te, tb, tfi, na: (0, te[t], 0, c)),
                  pl.BlockSpec((None, None, d, tf), lambda c, t, te, tb, tfi, na: (0, te[t], 0, c))],
        out_specs=pl.BlockSpec((tm, tf), lambda c, t, te, tb, tfi, na: (tb[t], c)),
        scratch_shapes=[pltpu.VMEM((d, tf), BF16), pltpu.VMEM((d, tf), BF16)])
    return pl.pallas_call(
        _moe_up_kernel,
        grid_spec=grid_spec,
        out_shape=jax.ShapeDtypeStruct((s, f), BF16),
        compiler_params=_cparams(56, 2),
        name="moe_gate_up",
    )(tile_expert, tile_blk, tile_first, n_active, xs, w_gate, w_up)


def _moe_down_kernel(te_ref, tb_ref, tf_ref, na_ref, h_ref, wd_ref, y_ref, wd_b):
    t = pl.program_id(1)

    @pl.when(tf_ref[t] == 1)
    def _():
        wd_b[...] = wd_ref[...].astype(BF16)

    @pl.when(t < na_ref[0])
    def _():
        y = jnp.dot(h_ref[...], wd_b[...], preferred_element_type=F32)
        for q in range(y_ref.shape[1]):
            y_ref[:, q, :] = y[:, q * LANES:(q + 1) * LANES]


def _moe_down(tile_expert, tile_blk, tile_first, n_active, h, w_down, tm):
    s, f = h.shape
    d = w_down.shape[3]
    tn = 1024
    nt = s // tm
    grid_spec = pltpu.PrefetchScalarGridSpec(
        num_scalar_prefetch=4,
        grid=(d // tn, nt),
        in_specs=[pl.BlockSpec((tm, f), lambda c, t, te, tb, tfi, na: (tb[t], 0)),
                  pl.BlockSpec((None, None, f, tn), lambda c, t, te, tb, tfi, na: (0, te[t], 0, c))],
        out_specs=pl.BlockSpec((tm, tn // LANES, LANES), lambda c, t, te, tb, tfi, na: (tb[t], c, 0)),
        scratch_shapes=[pltpu.VMEM((f, tn), BF16)])
    return pl.pallas_call(
        _moe_down_kernel,
        grid_spec=grid_spec,
        out_shape=jax.ShapeDtypeStruct((s, d // LANES, LANES), F32),
        compiler_params=_cparams(40, 2),
        name="moe_down",
    )(tile_expert, tile_blk, tile_first, n_active, h, w_down)


def _final_kernel(tm, row0, t_all, n_steps, pos_ref, y_ref, x1_ref, wt_ref, ga_ref, g_ref, o_ref, ybuf, fbuf, sem):
    i = pl.program_id(0)
    nk = y_ref.shape[1]

    def issue(step, slot):
        def body(r, carry):
            for k in range(TOP_K):
                p = pos_ref[k * t_all + row0 + step * tm + r]
                pltpu.make_async_copy(y_ref.at[p], ybuf.at[slot, k * tm + r], sem.at[slot]).start()
            return carry
        lax.fori_loop(0, tm, body, 0)

    @pl.when(i == 0)
    def _():
        issue(0, 0)

    @pl.when(i + 1 < n_steps)
    def _():
        issue(i + 1, (i + 1) % 2)

    slot = i % 2
    pltpu.make_async_copy(y_ref.at[pl.ds(0, TOP_K * tm)], ybuf.at[slot], sem.at[slot]).wait()
    wt = wt_ref[...]
    ss = jnp.zeros((tm, 1), F32)
    for q in range(nk):
        f = wt[:, 0:1] * ybuf[slot, pl.ds(0, tm), q, :]
        for k in range(1, TOP_K):
            f = f + wt[:, k:k + 1] * ybuf[slot, pl.ds(k * tm, tm), q, :]
        fbuf[:, q * LANES:(q + 1) * LANES] = f
        ss = ss + jnp.sum(f * f, axis=-1, keepdims=True)
    r = lax.rsqrt(ss / (nk * LANES) + NORM_EPS)
    o_ref[...] = x1_ref[...] + ga_ref[0] * (fbuf[...] * r * g_ref[...])


def _final(x1, y_slots, pos, wts, mod, g_post, row0, t_all):
    t, d = x1.shape
    tm = 128
    o0 = row0 // tm
    n_steps = t // tm
    nk = d // LANES
    grid_spec = pltpu.PrefetchScalarGridSpec(
        num_scalar_prefetch=1,
        grid=(n_steps,),
        in_specs=[pl.BlockSpec(memory_space=pl.ANY),
                  pl.BlockSpec((tm, d), lambda i, p: (i, 0)),
                  pl.BlockSpec((tm, LANES), lambda i, p: (o0 + i, 0)),
                  pl.BlockSpec((1, tm, d), lambda i, p: (0, i, 5)) if mod.per_row else
                  pl.BlockSpec((1, 1, d), lambda i, p: (i // (mod.rows_per_seq // tm), 0, 5)),
                  pl.BlockSpec((1, d), lambda i, p: (0, 0))],
        out_specs=pl.BlockSpec((tm, d), lambda i, p: (i, 0)),
        scratch_shapes=[pltpu.VMEM((2, TOP_K * tm, nk, LANES), F32), pltpu.VMEM((tm, d), F32),
                        pltpu.SemaphoreType.DMA((2,))])
    return pl.pallas_call(
        functools.partial(_final_kernel, tm, row0, t_all, n_steps),
        grid_spec=grid_spec,
        out_shape=jax.ShapeDtypeStruct((t, d), F32),
        compiler_params=_cparams(40, 1),
        name="gather_combine_residual",
    )(pos, y_slots, x1, wts, mod.arr, g_post.reshape(1, d))


def _routing_tables(eid, n_tok, tm, nt_max):
    e_flat = eid.reshape(-1)
    onehot = (e_flat[:, None] == jnp.arange(N_EXPERTS, dtype=jnp.int32)[None, :]).astype(jnp.int32)
    csum = jnp.cumsum(onehot, axis=0)
    rank = jnp.sum((csum - onehot) * onehot, axis=1)
    counts = csum[-1]
    tiles_e = (counts + tm - 1) // tm
    tile_end = jnp.cumsum(tiles_e)
    tile_start = tile_end - tiles_e
    n_active = tile_end[-1]
    slot = tile_start[e_flat] * tm + rank
    tile_ids = jnp.minimum(jnp.arange(nt_max, dtype=jnp.int32), n_active - 1)
    tile_expert = jnp.searchsorted(tile_end, tile_ids, side="right").astype(jnp.int32)
    tile_first = jnp.concatenate([jnp.ones((1,), jnp.int32),
                                  (tile_expert[1:] != tile_expert[:-1]).astype(jnp.int32)])
    tok = jnp.arange(2 * n_tok, dtype=jnp.int32) // 2
    src = jnp.zeros((nt_max * tm,), jnp.int32).at[slot].set(tok)
    pos = slot.reshape(n_tok, 2).T.reshape(-1)
    return (tile_expert, tile_ids.astype(jnp.int32), tile_first, n_active.reshape(1).astype(jnp.int32),
            src, pos.astype(jnp.int32))


def _rope_tables(pos, reps):
    half = QK_ROPE // 2
    freq = ROPE_THETA ** (-jnp.arange(half, dtype=F32) / half)
    ang = pos.astype(F32)[:, None] * freq[None, :]
    cos, sin = jnp.cos(ang), jnp.sin(ang)
    zeros = jnp.zeros((pos.shape[0], LANES - QK_ROPE), F32)
    cos_t = jnp.concatenate([cos, cos, zeros], axis=1)
    sin_t = jnp.concatenate([-sin, sin, zeros], axis=1)
    return jnp.tile(cos_t, (reps, 1)), jnp.tile(sin_t, (reps, 1))


def kernel(x_prompt, x_sample, c_prompt, c_sample, cache_kv_latent, cache_k_rope, state_conv, page_table, w_ada, b_ada, g_pre_mix, g_post_mix, g_pre_ffn, g_post_ffn, w_in, g_q_lat, g_kv_lat, w_uq, w_uk, w_uv, conv_w, conv_b, conv_ln_g, conv_ln_b, w_out, w_router_group, b_router_group, w_router_expert, b_router_expert, w_exp_gate, w_exp_up, w_exp_down):
    n_seq_p, seq_p, d = x_prompt.shape
    n_seq_s, seq_s, _ = x_sample.shape
    depth = w_ada.shape[0]
    assert depth == 1, "single-layer trunk"
    q_lora = g_q_lat.shape[1]
    kv_lora = g_kv_lat.shape[1]
    n_heads = w_uq.shape[2]
    conv_ch = conv_w.shape[2]
    past = page_table.shape[1] * cache_kv_latent.shape[2]
    t_p, t_s = n_seq_p * seq_p, n_seq_s * seq_s
    t_all = t_p + t_s
    l = 0

    w_in_l = w_in[l]
    sp = (q_lora, q_lora + kv_lora, q_lora + kv_lora + QK_ROPE, q_lora + kv_lora + QK_ROPE + conv_ch)
    w_in_b = jnp.concatenate(
        [w_in_l[:, sp[2]:sp[3]], w_in_l[:, sp[3]:], w_in_l[:, :sp[0]], w_in_l[:, sp[0]:sp[1]],
         w_in_l[:, sp[1]:sp[2]], jnp.zeros((d, LANES - QK_ROPE), F32)], axis=1).astype(BF16)
    qa_col = (2 * conv_ch) // q_lora
    kv_col = (2 * conv_ch + q_lora) // kv_lora
    kr_col = (2 * conv_ch + q_lora + kv_lora) // LANES
    wq_b = jnp.concatenate([w_uq[l], jnp.zeros((q_lora, n_heads, HEAD_PAD - QK_HEAD), F32)],
                           axis=2).reshape(q_lora, n_heads * HEAD_PAD).astype(BF16)
    wuk_b = w_uk[l].reshape(kv_lora, n_heads * QK_NOPE).astype(BF16)
    wuv_b = w_uv[l].reshape(kv_lora, n_heads * V_HEAD).astype(BF16)
    wukT_b = jnp.transpose(w_uk[l], (1, 2, 0)).astype(BF16)
    wuv_h_b = jnp.transpose(w_uv[l], (1, 0, 2)).astype(BF16)
    w_out_b = w_out[l].astype(BF16)
    wr_b = jnp.concatenate([w_router_group[l], w_router_expert[l],
                            jnp.zeros((d, LANES - N_EXPERT_GROUPS - N_EXPERTS), F32)], axis=1).astype(BF16)
    br = jnp.concatenate([b_router_group[l], b_router_expert[l],
                          jnp.zeros((LANES - N_EXPERT_GROUPS - N_EXPERTS,), F32)]).reshape(1, LANES)

    c_rows = n_seq_p + n_seq_s
    c_pad = -c_rows % 8
    c_all = jnp.concatenate([c_prompt, c_sample, jnp.zeros((c_pad, d), F32)], axis=0)
    mod_all = _ada(c_all, w_ada[l], b_ada[l])
    mod_p = _Mod(mod_all[:n_seq_p].reshape(n_seq_p, 1, 6 * d), seq_p, d)
    mod_s = _Mod(jnp.repeat(mod_all[n_seq_p:c_rows], seq_s, axis=0).reshape(1, t_s, 6 * d), seq_s, d)

    cos_p, sin_p = _rope_tables(jnp.arange(seq_p), 1)
    cos_s, sin_s = _rope_tables(past + jnp.arange(seq_s), n_seq_s)

    xp = x_prompt.reshape(t_p, d)
    xs = x_sample.reshape(t_s, d)

    z_p = _premix(xp, g_pre_mix[l], mod_p, w_in_b)
    kv_lat_p, k_rope_p, k_full_p, v_p = _kv_post(z_p, g_kv_lat[l], cos_p, sin_p, seq_p, kv_col, kr_col, wuk_b, wuv_b)
    q_full_p = _q_proj(z_p, g_q_lat[l], cos_p, sin_p, seq_p, qa_col, wq_b)
    attn_p = _flash(q_full_p.reshape(n_seq_p, seq_p, -1), k_full_p.reshape(n_seq_p, seq_p, -1),
                    v_p.reshape(n_seq_p, seq_p, -1), n_heads).reshape(t_p, n_heads * V_HEAD)
    conv_p, st_p = _conv_prompt(z_p, n_seq_p, seq_p, conv_ch, 0, conv_w[l], conv_b[l], conv_ln_g[l], conv_ln_b[l])
    x1_p, hp_p, lg_p = _outproj(attn_p, conv_p, w_out_b, xp, mod_p, g_post_mix[l], g_pre_ffn[l], wr_b, br)

    z_s = _premix(xs, g_pre_mix[l], mod_s, w_in_b)
    kv_lat_s, k_rope_s = _kv_post(z_s, g_kv_lat[l], cos_s, sin_s, t_s, kv_col, kr_col)
    q_full_s = _q_proj(z_s, g_q_lat[l], cos_s, sin_s, t_s, qa_col, wq_b)
    q_lat_s, q_rope_hs = _absorb(q_full_s, wukT_b)
    rows = seq_s * n_heads
    q_rope_s = jnp.transpose(q_rope_hs, (1, 0, 2)).reshape(n_seq_s, rows, QK_ROPE)
    o_lat_s = _paged_attn(page_table, q_lat_s.reshape(n_seq_s, rows, kv_lora), q_rope_s,
                          kv_lat_s.reshape(n_seq_s, seq_s, kv_lora), k_rope_s.reshape(n_seq_s, seq_s, QK_ROPE),
                          cache_kv_latent, jnp.swapaxes(cache_k_rope, 2, 3), n_heads)
    attn_s = _v_up(o_lat_s.reshape(t_s, n_heads * kv_lora), wuv_h_b)
    hist = state_conv.shape[2]
    state_s = state_conv.reshape(n_seq_s, hist, conv_ch)
    conv_s_t, u_s = _conv_sample(state_s, z_s.reshape(n_seq_s, seq_s, -1), conv_ch, 0,
                                 conv_w[l], conv_b[l], conv_ln_g[l], conv_ln_b[l])
    conv_s = jnp.transpose(conv_s_t, (1, 0, 2)).reshape(t_s, conv_ch)
    x1_s, hp_s, lg_s = _outproj(attn_s, conv_s, w_out_b, xs, mod_s, g_post_mix[l], g_pre_ffn[l], wr_b, br)

    hp_all = jnp.concatenate([hp_p, hp_s], axis=0)
    ids, wts = _router(jnp.concatenate([lg_p, lg_s], axis=0))
    tm_e = 256
    nt_max = (TOP_K * t_all) // tm_e + N_EXPERTS
    tile_expert, tile_blk, tile_first, n_active, src, pos = _routing_tables(ids[:, :TOP_K], t_all, tm_e, nt_max)
    xs_sorted = _token_gather(src, tile_blk, n_active, hp_all, tm_e, d)
    h_act = _moe_up(tile_expert, tile_blk, tile_first, n_active, xs_sorted, w_exp_gate, w_exp_up, tm_e)
    y_slots = _moe_down(tile_expert, tile_blk, tile_first, n_active, h_act, w_exp_down, tm_e)

    y_p = _final(x1_p, y_slots, pos, wts, mod_p, g_post_ffn[l], 0, t_all)
    y_s = _final(x1_s, y_slots, pos, wts, mod_s, g_post_ffn[l], t_p, t_all)

    conv_state_p = jnp.transpose(st_p, (0, 2, 1, 3)).reshape(n_seq_p, HALO, conv_ch)[:, HALO - (CONV_WIDTH - 1):, :]
    conv_state_s = jnp.concatenate([state_s[:, seq_s:, :], u_s], axis=1)
    return (y_p.reshape(n_seq_p, seq_p, d), y_s.reshape(n_seq_s, seq_s, d),
            kv_lat_p.reshape(1, n_seq_p, seq_p, kv_lora), k_rope_p.reshape(1, n_seq_p, seq_p, QK_ROPE),
            conv_state_p[None], kv_lat_s.reshape(1, n_seq_s, seq_s, kv_lora),
            k_rope_s.reshape(1, n_seq_s, seq_s, QK_ROPE), conv_state_s[None])
```

```python
import functools

import jax
import jax.numpy as jnp
from jax import lax
from jax.experimental import pallas as pl
from jax.experimental.pallas import tpu as pltpu

F32 = jnp.float32
BF16 = jnp.bfloat16

V_HEAD = 128
QK_NOPE = 128
QK_ROPE = 64
QK_HEAD = QK_NOPE + QK_ROPE
HEAD_PAD = 256
ROPE_THETA = 10000.0
SOFTMAX_SCALE = QK_HEAD ** -0.5
CONV_WIDTH = 31
N_EXPERT_GROUPS = 4
EXPERTS_PER_GROUP = 8
N_EXPERTS = N_EXPERT_GROUPS * EXPERTS_PER_GROUP
TOP_K = 2
NORM_EPS = 1e-6
LN_EPS = 1e-5
LANES = 128
HALO = 32

MIB = 1024 * 1024


def _cparams(vmem_mib, n_axes):
    return pltpu.CompilerParams(dimension_semantics=("arbitrary",) * n_axes,
                                vmem_limit_bytes=vmem_mib * MIB)


def _sigmoid(x):
    return 1.0 / (1.0 + jnp.exp(-x))


def _rms(x):
    return x * lax.rsqrt(jnp.mean(x * x, axis=-1, keepdims=True) + NORM_EPS)


def _rope_chunk(c, cos, sin):
    lane = lax.broadcasted_iota(jnp.int32, c.shape, 1)
    sw = jnp.where(lane < QK_ROPE // 2, pltpu.roll(c, LANES - QK_ROPE // 2, 1), pltpu.roll(c, QK_ROPE // 2, 1))
    return c * cos + sw * sin


def _ada_kernel(c_ref, w_ref, b_ref, o_ref):
    c = c_ref[...]
    a = (c * _sigmoid(c)).astype(BF16)
    o_ref[...] = jnp.dot(a, w_ref[...].astype(BF16), preferred_element_type=F32) + b_ref[...]


def _ada(c_all, w_ada, b_ada):
    m, d = c_all.shape
    n = w_ada.shape[1]
    tn = 512
    return pl.pallas_call(
        _ada_kernel,
        grid=(n // tn,),
        in_specs=[pl.BlockSpec((m, d), lambda j: (0, 0)),
                  pl.BlockSpec((d, tn), lambda j: (0, j)),
                  pl.BlockSpec((1, tn), lambda j: (0, j))],
        out_specs=pl.BlockSpec((m, tn), lambda j: (0, j)),
        out_shape=jax.ShapeDtypeStruct((m, n), F32),
        compiler_params=_cparams(40, 1),
        name="ada_ln",
    )(c_all, w_ada, b_ada.reshape(1, n))


class _Mod:
    def __init__(self, arr, rows_per_seq, d):
        self.arr = arr
        self.rows_per_seq = rows_per_seq
        self.d = d
        self.per_row = arr.shape[1] != 1

    def spec(self, tm, k, n_grid_axes):
        d = self.d
        if self.per_row:
            if n_grid_axes == 1:
                return pl.BlockSpec((1, tm, d), lambda i: (0, i, k))
            return pl.BlockSpec((1, tm, d), lambda i, j: (0, i, k))
        tiles = self.rows_per_seq // tm
        if n_grid_axes == 1:
            return pl.BlockSpec((1, 1, d), lambda i: (i // tiles, 0, k))
        return pl.BlockSpec((1, 1, d), lambda i, j: (i // tiles, 0, k))


def _premix_kernel(x_ref, g_ref, sc_ref, sh_ref, w_ref, o_ref, h_ref):
    @pl.when(pl.program_id(1) == 0)
    def _():
        h = _rms(x_ref[...]) * g_ref[...]
        h = h * (1.0 + sc_ref[0]) + sh_ref[0]
        h_ref[...] = h.astype(BF16)

    o_ref[...] = jnp.dot(h_ref[...], w_ref[...], preferred_element_type=F32)


def _premix(x, g, mod, w_b):
    t, d = x.shape
    n = w_b.shape[1]
    tm, tn = (128 if mod.per_row else 512), 640
    return pl.pallas_call(
        _premix_kernel,
        grid=(t // tm, n // tn),
        in_specs=[pl.BlockSpec((tm, d), lambda i, j: (i, 0)),
                  pl.BlockSpec((1, d), lambda i, j: (0, 0)),
                  mod.spec(tm, 1, 2), mod.spec(tm, 0, 2),
                  pl.BlockSpec((d, tn), lambda i, j: (0, j))],
        out_specs=pl.BlockSpec((tm, tn), lambda i, j: (i, j)),
        out_shape=jax.ShapeDtypeStruct((t, n), F32),
        scratch_shapes=[pltpu.VMEM((tm, d), BF16)],
        compiler_params=_cparams(48, 2),
        name="premix_in_proj",
    )(x, g.reshape(1, d), mod.arr, mod.arr, w_b)


def _kv_kernel(with_up, n_heads, kva_ref, kr_ref, g_ref, cos_ref, sin_ref, *rest):
    if with_up:
        wuk_ref, wuv_ref, kv_out, kr_out, kfull_out, v_out = rest
    else:
        kv_out, kr_out = rest
    kv = _rms(kva_ref[...]) * g_ref[...]
    kv_out[...] = kv
    rot = _rope_chunk(kr_ref[...], cos_ref[...], sin_ref[...])
    kr_out[...] = rot[:, :QK_ROPE]
    if with_up:
        kvb = kv.astype(BF16)
        k_nope = jnp.dot(kvb, wuk_ref[...], preferred_element_type=F32)
        v_out[...] = jnp.dot(kvb, wuv_ref[...], preferred_element_type=F32).astype(BF16)
        rot_b = rot.astype(BF16)
        for h in range(n_heads):
            kfull_out[:, h * HEAD_PAD:h * HEAD_PAD + QK_NOPE] = k_nope[:, h * QK_NOPE:(h + 1) * QK_NOPE].astype(BF16)
            kfull_out[:, h * HEAD_PAD + QK_NOPE:(h + 1) * HEAD_PAD] = rot_b


def _kv_post(z, g_kv, cos_t, sin_t, pos_rows, kv_col, kr_col, wuk_b=None, wuv_b=None):
    t = z.shape[0]
    kv_lora = g_kv.shape[0]
    tm = 256
    pos_tiles = pos_rows // tm
    with_up = wuk_b is not None
    in_specs = [pl.BlockSpec((tm, kv_lora), lambda i: (i, kv_col)),
                pl.BlockSpec((tm, LANES), lambda i: (i, kr_col)),
                pl.BlockSpec((1, kv_lora), lambda i: (0, 0)),
                pl.BlockSpec((tm, LANES), lambda i: (i % pos_tiles, 0)),
                pl.BlockSpec((tm, LANES), lambda i: (i % pos_tiles, 0))]
    args = [z, z, g_kv.reshape(1, kv_lora), cos_t, sin_t]
    out_specs = [pl.BlockSpec((tm, kv_lora), lambda i: (i, 0)),
                 pl.BlockSpec((tm, QK_ROPE), lambda i: (i, 0))]
    out_shape = [jax.ShapeDtypeStruct((t, kv_lora), F32), jax.ShapeDtypeStruct((t, QK_ROPE), F32)]
    n_heads = 0
    if with_up:
        n_heads = wuk_b.shape[1] // QK_NOPE
        in_specs += [pl.BlockSpec(wuk_b.shape, lambda i: (0, 0)), pl.BlockSpec(wuv_b.shape, lambda i: (0, 0))]
        args += [wuk_b, wuv_b]
        out_specs += [pl.BlockSpec((tm, n_heads * HEAD_PAD), lambda i: (i, 0)),
                      pl.BlockSpec((tm, n_heads * V_HEAD), lambda i: (i, 0))]
        out_shape += [jax.ShapeDtypeStruct((t, n_heads * HEAD_PAD), BF16),
                      jax.ShapeDtypeStruct((t, n_heads * V_HEAD), BF16)]
    return pl.pallas_call(
        functools.partial(_kv_kernel, with_up, n_heads),
        grid=(t // tm,),
        in_specs=in_specs, out_specs=out_specs, out_shape=out_shape,
        compiler_params=_cparams(32, 1),
        name="kv_post_up" if with_up else "kv_post",
    )(*args)


def _q_kernel(heads_per_step, qa_ref, g_ref, cos_ref, sin_ref, w_ref, o_ref):
    qn = (_rms(qa_ref[...]) * g_ref[...]).astype(BF16)
    q = jnp.dot(qn, w_ref[...], preferred_element_type=F32)
    cos = cos_ref[...]
    sin = sin_ref[...]
    for h in range(heads_per_step):
        lo = h * HEAD_PAD
        o_ref[:, lo:lo + QK_NOPE] = q[:, lo:lo + QK_NOPE].astype(BF16)
        o_ref[:, lo + QK_NOPE:lo + HEAD_PAD] = _rope_chunk(q[:, lo + QK_NOPE:lo + HEAD_PAD], cos, sin).astype(BF16)


def _q_proj(z, g_q, cos_t, sin_t, pos_rows, qa_col, wq_b):
    t = z.shape[0]
    q_lora, n = wq_b.shape
    tm, tn = 256, 1024
    pos_tiles = pos_rows // tm
    return pl.pallas_call(
        functools.partial(_q_kernel, tn // HEAD_PAD),
        grid=(t // tm, n // tn),
        in_specs=[pl.BlockSpec((tm, q_lora), lambda i, j: (i, qa_col)),
                  pl.BlockSpec((1, q_lora), lambda i, j: (0, 0)),
                  pl.BlockSpec((tm, LANES), lambda i, j: (i % pos_tiles, 0)),
                  pl.BlockSpec((tm, LANES), lambda i, j: (i % pos_tiles, 0)),
                  pl.BlockSpec((q_lora, tn), lambda i, j: (0, j))],
        out_specs=pl.BlockSpec((tm, tn), lambda i, j: (i, j)),
        out_shape=jax.ShapeDtypeStruct((t, n), BF16),
        compiler_params=_cparams(32, 2),
        name="q_proj_rope",
    )(z, g_q.reshape(1, q_lora), cos_t, sin_t, wq_b)


def _flash_kernel(tb, g_heads, it_ref, jt_ref, q_ref, k_ref, v_ref, o_ref, m_ref, l_ref, acc_ref):
    pair = pl.program_id(2)
    i = it_ref[pair]
    j = jt_ref[pair]

    @pl.when(j == 0)
    def _():
        m_ref[...] = jnp.full(m_ref.shape, -jnp.inf, F32)
        l_ref[...] = jnp.zeros(l_ref.shape, F32)
        acc_ref[...] = jnp.zeros(acc_ref.shape, F32)

    def step(diagonal):
        for g in range(g_heads):
            q = q_ref[:, g * HEAD_PAD:(g + 1) * HEAD_PAD]
            k = k_ref[:, g * HEAD_PAD:(g + 1) * HEAD_PAD]
            s = lax.dot_general(q, k, (((1,), (1,)), ((), ())), preferred_element_type=F32) * SOFTMAX_SCALE
            if diagonal:
                row = lax.broadcasted_iota(jnp.int32, s.shape, 0)
                col = lax.broadcasted_iota(jnp.int32, s.shape, 1)
                s = jnp.where(col <= row, s, -jnp.inf)
            m_prev = m_ref[g]
            m_new = jnp.maximum(m_prev, jnp.max(s, axis=-1, keepdims=True))
            alpha = jnp.exp(m_prev - m_new)
            p = jnp.exp(s - m_new)
            l_ref[g] = alpha * l_ref[g] + jnp.sum(p, axis=-1, keepdims=True)
            acc_ref[g] = alpha * acc_ref[g] + jnp.dot(p.astype(BF16), v_ref[:, g * V_HEAD:(g + 1) * V_HEAD],
                                                      preferred_element_type=F32)
            m_ref[g] = m_new

    @pl.when(j < i)
    def _():
        step(False)

    @pl.when(j == i)
    def _():
        step(True)
        for g in range(g_heads):
            o_ref[:, g * V_HEAD:(g + 1) * V_HEAD] = (acc_ref[g] / l_ref[g]).astype(o_ref.dtype)


def _flash(q_full, k_full, v, n_heads):
    b, t, _ = q_full.shape
    tb = 512
    g_heads = 4
    nb = t // tb
    pairs = [(i, j) for i in range(nb) for j in range(i + 1)]
    it = jnp.asarray([p[0] for p in pairs], jnp.int32)
    jt = jnp.asarray([p[1] for p in pairs], jnp.int32)
    grid_spec = pltpu.PrefetchScalarGridSpec(
        num_scalar_prefetch=2,
        grid=(b, n_heads // g_heads, len(pairs)),
        in_specs=[pl.BlockSpec((None, tb, g_heads * HEAD_PAD), lambda bb, h, p, it_r, jt_r: (bb, it_r[p], h)),
                  pl.BlockSpec((None, tb, g_heads * HEAD_PAD), lambda bb, h, p, it_r, jt_r: (bb, jt_r[p], h)),
                  pl.BlockSpec((None, tb, g_heads * V_HEAD), lambda bb, h, p, it_r, jt_r: (bb, jt_r[p], h))],
        out_specs=pl.BlockSpec((None, tb, g_heads * V_HEAD), lambda bb, h, p, it_r, jt_r: (bb, it_r[p], h)),
        scratch_shapes=[pltpu.VMEM((g_heads, tb, 1), F32), pltpu.VMEM((g_heads, tb, 1), F32),
                        pltpu.VMEM((g_heads, tb, V_HEAD), F32)])
    return pl.pallas_call(
        functools.partial(_flash_kernel, tb, g_heads),
        grid_spec=grid_spec,
        out_shape=jax.ShapeDtypeStruct((b, t, n_heads * V_HEAD), BF16),
        compiler_params=_cparams(32, 3),
        name="prompt_flash_attn",
    )(it, jt, q_full, k_full, v)


def _absorb_kernel(q_ref, w_ref, ql_ref, qr_ref):
    q = q_ref[...]
    ql_ref[...] = jnp.dot(q[:, :QK_NOPE], w_ref[...], preferred_element_type=F32).astype(BF16)
    qr_ref[...] = q[:, QK_NOPE:QK_NOPE + QK_ROPE]


def _absorb(q_full, wukT_b):
    t = q_full.shape[0]
    n_heads, _, kv_lora = wukT_b.shape
    return pl.pallas_call(
        _absorb_kernel,
        grid=(n_heads,),
        in_specs=[pl.BlockSpec((t, HEAD_PAD), lambda h: (0, h)),
                  pl.BlockSpec((None, QK_NOPE, kv_lora), lambda h: (h, 0, 0))],
        out_specs=[pl.BlockSpec((t, kv_lora), lambda h: (0, h)),
                   pl.BlockSpec((None, t, QK_ROPE), lambda h: (h, 0, 0))],
        out_shape=[jax.ShapeDtypeStruct((t, n_heads * kv_lora), BF16),
                   jax.ShapeDtypeStruct((n_heads, t, QK_ROPE), BF16)],
        compiler_params=_cparams(32, 1),
        name="sample_absorb_q",
    )(q_full, wukT_b)


def _paged_kernel(pages, page_size, n_seq, n_chunks, n_heads, pt_ref, ql_ref, qr_ref, kvn_ref, krn_ref,
                  ckv_ref, ckr_ref, o_ref, kvbuf, krbuf, kbuf, rbuf, m_ref, l_ref, acc_ref, sem):
    b = pl.program_id(0)
    c = pl.program_id(1)
    g = b * n_chunks + c

    def issue(step, slot):
        bb = step // n_chunks
        cc = step % n_chunks
        for p in range(pages):
            page = pt_ref[bb, cc * pages + p]
            pltpu.make_async_copy(ckv_ref.at[0, page], kvbuf.at[slot, p], sem.at[0, slot]).start()
            pltpu.make_async_copy(ckr_ref.at[0, page], krbuf.at[slot, p], sem.at[1, slot]).start()

    @pl.when(g == 0)
    def _():
        issue(0, 0)

    @pl.when(g + 1 < n_seq * n_chunks)
    def _():
        issue(g + 1, (g + 1) % 2)

    slot = g % 2
    pltpu.make_async_copy(ckv_ref.at[0, pl.ds(0, pages)], kvbuf.at[slot], sem.at[0, slot]).wait()
    pltpu.make_async_copy(ckr_ref.at[0, pl.ds(0, pages)], krbuf.at[slot], sem.at[1, slot]).wait()
    kbufs = (kbuf,)
    rbufs = (rbuf,)
    kv_refs = [kvbuf.at[slot, p] for p in range(pages)]
    kr_refs = [krbuf.at[slot, p] for p in range(pages)]

    @pl.when(c == 0)
    def _():
        m_ref[...] = jnp.full(m_ref.shape, -jnp.inf, F32)
        l_ref[...] = jnp.zeros(l_ref.shape, F32)
        acc_ref[...] = jnp.zeros(acc_ref.shape, F32)

    ql = ql_ref[...]
    qr = qr_ref[...]
    nt = (((1,), (1,)), ((), ()))
    m = m_ref[...]
    l = l_ref[...]
    acc = acc_ref[...]
    per = pages // len(kbufs)
    for sc, (kbuf, rbuf) in enumerate(zip(kbufs, rbufs)):
        for p in range(per):
            pg = sc * per + p
            kbuf[p * page_size:(p + 1) * page_size, :] = kv_refs[pg][...].astype(BF16)
            rbuf[:, p * page_size:(p + 1) * page_size] = kr_refs[pg][...].astype(BF16)
        kb = kbuf[...]
        s = (lax.dot_general(ql, kb, nt, preferred_element_type=F32)
             + jnp.dot(qr, rbuf[...], preferred_element_type=F32)) * SOFTMAX_SCALE
        m_new = jnp.maximum(m, jnp.max(s, axis=-1, keepdims=True))
        alpha = jnp.exp(m - m_new)
        p_ = jnp.exp(s - m_new)
        l = alpha * l + jnp.sum(p_, axis=-1, keepdims=True)
        acc = alpha * acc + jnp.dot(p_.astype(BF16), kb, preferred_element_type=F32)
        m = m_new
    m_ref[...] = m
    l_ref[...] = l
    acc_ref[...] = acc

    @pl.when(c == n_chunks - 1)
    def _():
        qlf = ql.astype(F32)
        qrf = qr.astype(F32)
        kvn = kvn_ref[...].astype(BF16).astype(F32)
        krn = krn_ref[...].astype(BF16).astype(F32)
        n_new = kvn.shape[0]
        tok = lax.broadcasted_iota(jnp.int32, (ql.shape[0], 1), 0) // n_heads
        s_new = []
        for jn in range(n_new):
            sj = (jnp.sum(qlf * kvn[jn:jn + 1, :], axis=-1, keepdims=True)
                  + jnp.sum(qrf * krn[jn:jn + 1, :], axis=-1, keepdims=True)) * SOFTMAX_SCALE
            s_new.append(jnp.where(tok >= jn, sj, -jnp.inf))
        m_prev2 = m_ref[...]
        m_fin = m_prev2
        for sj in s_new:
            m_fin = jnp.maximum(m_fin, sj)
        alpha2 = jnp.exp(m_prev2 - m_fin)
        l_fin = alpha2 * l_ref[...]
        acc = alpha2 * acc_ref[...]
        for jn, sj in enumerate(s_new):
            pj = jnp.exp(sj - m_fin)
            l_fin = l_fin + pj
            acc = acc + pj.astype(BF16).astype(F32) * kvn[jn:jn + 1, :]
        o_ref[...] = (acc / l_fin).astype(o_ref.dtype)


def _paged_attn(page_table, q_lat, q_rope, kv_new, kr_new, cache_kv, cache_kr_t, n_heads):
    n_seq, rows, kv_lora = q_lat.shape
    page_size = cache_kv.shape[2]
    n_pages = page_table.shape[1]
    n_new = kv_new.shape[1]
    pages = 32
    n_chunks = n_pages // pages
    in_specs = [pl.BlockSpec((None, rows, kv_lora), lambda b, c, pt: (b, 0, 0)),
                pl.BlockSpec((None, rows, QK_ROPE), lambda b, c, pt: (b, 0, 0)),
                pl.BlockSpec((None, n_new, kv_lora), lambda b, c, pt: (b, 0, 0)),
                pl.BlockSpec((None, n_new, QK_ROPE), lambda b, c, pt: (b, 0, 0)),
                pl.BlockSpec(memory_space=pl.ANY),
                pl.BlockSpec(memory_space=pl.ANY)]
    grid_spec = pltpu.PrefetchScalarGridSpec(
        num_scalar_prefetch=1,
        grid=(n_seq, n_chunks),
        in_specs=in_specs,
        out_specs=pl.BlockSpec((None, rows, kv_lora), lambda b, c, pt: (b, 0, 0)),
        scratch_shapes=[pltpu.VMEM((2, pages, page_size, kv_lora), F32),
                        pltpu.VMEM((2, pages, QK_ROPE, page_size), F32),
                        pltpu.VMEM((pages * page_size, kv_lora), BF16),
                        pltpu.VMEM((QK_ROPE, pages * page_size), BF16),
                        pltpu.VMEM((rows, 1), F32), pltpu.VMEM((rows, 1), F32),
                        pltpu.VMEM((rows, kv_lora), F32),
                        pltpu.SemaphoreType.DMA((2, 2))])
    return pl.pallas_call(
        functools.partial(_paged_kernel, pages, page_size, n_seq, n_chunks, n_heads),
        grid_spec=grid_spec,
        out_shape=jax.ShapeDtypeStruct((n_seq, rows, kv_lora), BF16),
        compiler_params=_cparams(40, 2),
        name="sample_paged_attn",
    )(page_table, q_lat, q_rope, kv_new, kr_new, cache_kv, cache_kr_t)


def _vup_kernel(o_ref, w_ref, a_ref):
    a_ref[...] = jnp.dot(o_ref[...], w_ref[...], preferred_element_type=F32).astype(BF16)


def _v_up(o_lat, wuv_h_b):
    t = o_lat.shape[0]
    n_heads, kv_lora, _ = wuv_h_b.shape
    return pl.pallas_call(
        _vup_kernel,
        grid=(n_heads,),
        in_specs=[pl.BlockSpec((t, kv_lora), lambda h: (0, h)),
                  pl.BlockSpec((None, kv_lora, V_HEAD), lambda h: (h, 0, 0))],
        out_specs=pl.BlockSpec((t, V_HEAD), lambda h: (0, h)),
        out_shape=jax.ShapeDtypeStruct((t, n_heads * V_HEAD), BF16),
        compiler_params=_cparams(32, 1),
        name="sample_v_up",
    )(o_lat, wuv_h_b)


def _ln_swish(y, g, b):
    mu = jnp.mean(y, axis=-1, keepdims=True)
    yc = y - mu
    var = jnp.mean(yc * yc, axis=-1, keepdims=True)
    o = yc * lax.rsqrt(var + LN_EPS) * g + b
    return o * _sigmoid(o)


def _conv_prompt_kernel(tt, cw, n_cc, val_ref, gate_ref, hval_ref, hgate_ref, w_ref, b_ref, g_ref, lb_ref,
                        o_ref, st_ref, ext_ref, y_ref):
    i = pl.program_id(1)
    cc = pl.program_id(2)
    um = val_ref[...] * _sigmoid(gate_ref[...])
    uh = hval_ref[...] * _sigmoid(hgate_ref[...])
    uh = jnp.where(i == 0, 0.0, uh)
    ext_ref[0:HALO, :] = uh
    ext_ref[HALO:HALO + tt, :] = um
    st_ref[cc] = um[tt - HALO:, :]
    rb = 64
    w = w_ref[...]
    bias = b_ref[...]
    for r in range(0, tt, rb):
        acc = jnp.zeros((rb, cw), F32) + bias
        for k in range(CONV_WIDTH):
            s0 = r + k + HALO - (CONV_WIDTH - 1)
            acc = acc + w[k:k + 1, :] * ext_ref[s0:s0 + rb, :]
        y_ref[cc, r:r + rb, :] = acc

    @pl.when(cc == n_cc - 1)
    def _():
        s1 = jnp.zeros((tt, 1), F32)
        for c2 in range(n_cc):
            s1 = s1 + jnp.sum(y_ref[c2], axis=-1, keepdims=True)
        mu = s1 / (n_cc * cw)
        s2 = jnp.zeros((tt, 1), F32)
        for c2 in range(n_cc):
            d = y_ref[c2] - mu
            s2 = s2 + jnp.sum(d * d, axis=-1, keepdims=True)
        rs = lax.rsqrt(s2 / (n_cc * cw) + LN_EPS)
        for c2 in range(n_cc):
            o = (y_ref[c2] - mu) * rs * g_ref[:, c2 * cw:(c2 + 1) * cw] + lb_ref[:, c2 * cw:(c2 + 1) * cw]
            o_ref[:, c2 * cw:(c2 + 1) * cw] = (o * _sigmoid(o)).astype(o_ref.dtype)


def _conv_prompt(z, n_seq, seq, conv_ch, val_col0, conv_w, conv_b, ln_g, ln_b):
    tt, cw = 256, 256
    n_cc = conv_ch // cw
    nt = seq // tt
    hb = tt // HALO
    gate_off = conv_ch // cw

    def main(off):
        return pl.BlockSpec((tt, cw), lambda b, i, c: (b * nt + i, val_col0 + off + c))

    def halo(off):
        return pl.BlockSpec((HALO, cw), lambda b, i, c: (jnp.maximum((b * nt + i) * hb - 1, 0), val_col0 + off + c))

    return pl.pallas_call(
        functools.partial(_conv_prompt_kernel, tt, cw, n_cc),
        grid=(n_seq, nt, n_cc),
        in_specs=[main(0), main(gate_off), halo(0), halo(gate_off),
                  pl.BlockSpec((CONV_WIDTH, cw), lambda b, i, c: (0, c)),
                  pl.BlockSpec((1, cw), lambda b, i, c: (0, c)),
                  pl.BlockSpec((1, conv_ch), lambda b, i, c: (0, 0)),
                  pl.BlockSpec((1, conv_ch), lambda b, i, c: (0, 0))],
        out_specs=[pl.BlockSpec((tt, conv_ch), lambda b, i, c: (b * nt + i, 0)),
                   pl.BlockSpec((None, n_cc, HALO, cw), lambda b, i, c: (b, 0, 0, 0))],
        out_shape=[jax.ShapeDtypeStruct((n_seq * seq, conv_ch), BF16),
                   jax.ShapeDtypeStruct((n_seq, n_cc, HALO, cw), F32)],
        scratch_shapes=[pltpu.VMEM((HALO + tt, cw), F32), pltpu.VMEM((n_cc, tt, cw), F32)],
        compiler_params=_cparams(32, 3),
        name="conv_prompt",
    )(z, z, z, z, conv_w, conv_b.reshape(1, conv_ch), ln_g.reshape(1, conv_ch), ln_b.reshape(1, conv_ch))


def _conv_sample_kernel(n_new, st_ref, val_ref, gate_ref, wst_ref, wu_ref, b_ref, g_ref, lb_ref, o_ref, u_ref):
    u = val_ref[...] * _sigmoid(gate_ref[...])
    u_ref[...] = u
    st = st_ref[...]
    for t in range(n_new):
        y = (jnp.sum(st * wst_ref[t][None], axis=1) + jnp.sum(u * wu_ref[t][None], axis=1) + b_ref[...])
        o_ref[t] = _ln_swish(y, g_ref[...], lb_ref[...]).astype(o_ref.dtype)


def _conv_sample(state, z3, conv_ch, val_col0, conv_w, conv_b, ln_g, ln_b):
    n_seq, hist, _ = state.shape
    n_new = z3.shape[1]
    nb = 16
    jj = jnp.arange(hist)[None, :] - jnp.arange(n_new)[:, None]
    wst = jnp.where((jj >= 0)[..., None], conv_w[jnp.clip(jj, 0, CONV_WIDTH - 1)], 0.0)
    ii = hist - jnp.arange(n_new)[:, None] + jnp.arange(n_new)[None, :]
    wu = jnp.where((ii <= CONV_WIDTH - 1)[..., None], conv_w[jnp.clip(ii, 0, CONV_WIDTH - 1)], 0.0)
    gate_off = 1
    return pl.pallas_call(
        functools.partial(_conv_sample_kernel, n_new),
        grid=(n_seq // nb,),
        in_specs=[pl.BlockSpec((nb, hist, conv_ch), lambda i: (i, 0, 0)),
                  pl.BlockSpec((nb, n_new, conv_ch), lambda i: (i, 0, val_col0)),
                  pl.BlockSpec((nb, n_new, conv_ch), lambda i: (i, 0, val_col0 + gate_off)),
                  pl.BlockSpec((n_new, hist, conv_ch), lambda i: (0, 0, 0)),
                  pl.BlockSpec((n_new, n_new, conv_ch), lambda i: (0, 0, 0)),
                  pl.BlockSpec((1, conv_ch), lambda i: (0, 0)),
                  pl.BlockSpec((1, conv_ch), lambda i: (0, 0)),
                  pl.BlockSpec((1, conv_ch), lambda i: (0, 0))],
        out_specs=[pl.BlockSpec((n_new, nb, conv_ch), lambda i: (0, i, 0)),
                   pl.BlockSpec((nb, n_new, conv_ch), lambda i: (i, 0, 0))],
        out_shape=[jax.ShapeDtypeStruct((n_new, n_seq, conv_ch), BF16),
                   jax.ShapeDtypeStruct((n_seq, n_new, conv_ch), F32)],
        compiler_params=_cparams(32, 1),
        name="conv_sample",
    )(state, z3, z3, wst, wu, conv_b.reshape(1, conv_ch), ln_g.reshape(1, conv_ch), ln_b.reshape(1, conv_ch))


def _pack_pair(lo, hi):
    lo_u = lax.bitcast_convert_type(lo.astype(BF16).astype(F32), jnp.uint32)
    hi_u = lax.bitcast_convert_type(hi.astype(BF16).astype(F32), jnp.uint32)
    return (lo_u >> 16) | (hi_u & jnp.uint32(0xFFFF0000))


def _unpack_pair(w):
    lo = lax.bitcast_convert_type(w << 16, F32).astype(BF16)
    hi = lax.bitcast_convert_type(w & jnp.uint32(0xFFFF0000), F32).astype(BF16)
    return lo, hi


def _outproj_kernel(nj, tn, a_ref, c_ref, wt_ref, wb_ref, x_ref, ga_ref, gpost_ref, gpre_ref, sc_ref, sh_ref,
                    wr_ref, br_ref, x1_ref, hp_ref, lg_ref, mix_ref):
    j = pl.program_id(1)
    mix_ref[j] = (jnp.dot(a_ref[...], wt_ref[...], preferred_element_type=F32)
                  + jnp.dot(c_ref[...], wb_ref[...], preferred_element_type=F32))

    @pl.when(j == nj - 1)
    def _():
        tm = x_ref.shape[0]
        d = nj * tn
        ss = jnp.zeros((tm, 1), F32)
        for c in range(nj):
            m = mix_ref[c]
            ss = ss + jnp.sum(m * m, axis=-1, keepdims=True)
        r = lax.rsqrt(ss / d + NORM_EPS)
        ss1 = jnp.zeros((tm, 1), F32)
        for c in range(nj):
            cs = slice(c * tn, (c + 1) * tn)
            x1 = x_ref[:, cs] + ga_ref[0, :, cs] * (mix_ref[c] * r * gpost_ref[:, cs])
            x1_ref[:, cs] = x1
            ss1 = ss1 + jnp.sum(x1 * x1, axis=-1, keepdims=True)
        r1 = lax.rsqrt(ss1 / d + NORM_EPS)
        half = nj // 2
        lg = jnp.zeros(lg_ref.shape, F32) + br_ref[...]
        for c in range(half):
            parts = []
            for cc in (c, c + half):
                cs = slice(cc * tn, (cc + 1) * tn)
                h = (x1_ref[:, cs] * r1 * gpre_ref[:, cs]) * (1.0 + sc_ref[0, :, cs]) + sh_ref[0, :, cs]
                lg = lg + jnp.dot(h.astype(BF16), wr_ref[cs, :], preferred_element_type=F32)
                parts.append(h)
            packed = _pack_pair(parts[0], parts[1])
            ks = d // 2 // LANES
            for q in range(tn // LANES):
                k = c * (tn // LANES) + q
                hp_ref[pl.ds(k, tm, stride=ks), :] = packed[:, q * LANES:(q + 1) * LANES]
        lg_ref[...] = lg


def _outproj(attn, conv, w_out_b, x, mod, g_post, g_pre, wr_b, br):
    t, d = x.shape
    kh = attn.shape[1]
    tm, tn = (128 if mod.per_row else 256), 512
    nj = d // tn
    ks = d // 2 // LANES
    return pl.pallas_call(
        functools.partial(_outproj_kernel, nj, tn),
        grid=(t // tm, nj),
        in_specs=[pl.BlockSpec((tm, kh), lambda i, j: (i, 0)),
                  pl.BlockSpec((tm, kh), lambda i, j: (i, 0)),
                  pl.BlockSpec((kh, tn), lambda i, j: (0, j)),
                  pl.BlockSpec((kh, tn), lambda i, j: (1, j)),
                  pl.BlockSpec((tm, d), lambda i, j: (i, 0)),
                  mod.spec(tm, 2, 2),
                  pl.BlockSpec((1, d), lambda i, j: (0, 0)),
                  pl.BlockSpec((1, d), lambda i, j: (0, 0)),
                  mod.spec(tm, 4, 2), mod.spec(tm, 3, 2),
                  pl.BlockSpec((d, LANES), lambda i, j: (0, 0)),
                  pl.BlockSpec((1, LANES), lambda i, j: (0, 0))],
        out_specs=[pl.BlockSpec((tm, d), lambda i, j: (i, 0)),
                   pl.BlockSpec((tm * ks, LANES), lambda i, j: (i, 0)),
                   pl.BlockSpec((tm, LANES), lambda i, j: (i, 0))],
        out_shape=[jax.ShapeDtypeStruct((t, d), F32),
                   jax.ShapeDtypeStruct((t * ks, LANES), jnp.uint32),
                   jax.ShapeDtypeStruct((t, LANES), F32)],
        scratch_shapes=[pltpu.VMEM((nj, tm, tn), F32)],
        compiler_params=_cparams(48, 2),
        name="out_proj_residual_prenorm",
    )(attn, conv, w_out_b, w_out_b, x, mod.arr, g_post.reshape(1, d), g_pre.reshape(1, d), mod.arr, mod.arr,
      wr_b, br)


def _router_kernel(lg_ref, id_ref, wt_ref):
    lg = lg_ref[...]
    lane = lax.broadcasted_iota(jnp.int32, lg.shape, 1)
    is_g = lane < N_EXPERT_GROUPS
    gmax = jnp.max(jnp.where(is_g, lg, -jnp.inf), axis=-1, keepdims=True)
    gexp = jnp.where(is_g, jnp.exp(jnp.where(is_g, lg, gmax) - gmax), 0.0)
    gprob = gexp / jnp.sum(gexp, axis=-1, keepdims=True)
    gw = jnp.max(gprob, axis=-1, keepdims=True)
    gidx = jnp.min(jnp.where(is_g & (gprob == gw), lane, LANES), axis=-1, keepdims=True)
    lo = N_EXPERT_GROUPS + gidx * EXPERTS_PER_GROUP
    is_e = (lane >= lo) & (lane < lo + EXPERTS_PER_GROUP)
    emax = jnp.max(jnp.where(is_e, lg, -jnp.inf), axis=-1, keepdims=True)
    eexp = jnp.where(is_e, jnp.exp(jnp.where(is_e, lg, emax) - emax), 0.0)
    ep = jnp.where(is_e, eexp / jnp.sum(eexp, axis=-1, keepdims=True), -1.0)
    p1 = jnp.max(ep, axis=-1, keepdims=True)
    i1 = jnp.min(jnp.where(ep == p1, lane, LANES), axis=-1, keepdims=True)
    ep2 = jnp.where(lane == i1, -1.0, ep)
    p2 = jnp.max(ep2, axis=-1, keepdims=True)
    i2 = jnp.min(jnp.where(ep2 == p2, lane, LANES), axis=-1, keepdims=True)
    den = p1 + p2
    w1 = p1 / den * gw
    w2 = p2 / den * gw
    id_ref[...] = jnp.where(lane == 0, i1 - N_EXPERT_GROUPS, jnp.where(lane == 1, i2 - N_EXPERT_GROUPS, 0))
    wt_ref[...] = jnp.where(lane == 0, w1, jnp.where(lane == 1, w2, 0.0))


def _router(logits):
    t = logits.shape[0]
    tm = 512
    spec = pl.BlockSpec((tm, LANES), lambda i: (i, 0))
    return pl.pallas_call(
        _router_kernel,
        grid=(t // tm,),
        in_specs=[spec], out_specs=[spec, spec],
        out_shape=[jax.ShapeDtypeStruct((t, LANES), jnp.int32), jax.ShapeDtypeStruct((t, LANES), F32)],
        compiler_params=_cparams(32, 1),
        name="router_topk",
    )(logits)


def _token_gather_kernel(tm, ks, src_ref, tb_ref, na_ref, hp_ref, o_ref, buf, sem):
    t = pl.program_id(0)
    na = na_ref[0]

    def issue(tile, slot):
        def body(r, carry):
            tok = src_ref[tile * tm + r]
            pltpu.make_async_copy(hp_ref.at[pl.ds(pl.multiple_of(tok * ks, ks), ks)],
                                  buf.at[slot, pl.ds(pl.multiple_of(r * ks, ks), ks)],
                                  sem.at[slot]).start()
            return carry
        lax.fori_loop(0, tm, body, 0, unroll=8)

    @pl.when(t == 0)
    def _():
        issue(0, 0)

    @pl.when(t + 1 < na)
    def _():
        issue(t + 1, (t + 1) % 2)

    @pl.when(t < na)
    def _():
        slot = t % 2
        pltpu.make_async_copy(hp_ref.at[pl.ds(0, tm * ks)], buf.at[slot], sem.at[slot]).wait()
        kh = ks * LANES
        for k in range(ks):
            lo, hi = _unpack_pair(buf[slot, pl.ds(k, tm, stride=ks), :])
            o_ref[:, k * LANES:(k + 1) * LANES] = lo
            o_ref[:, kh + k * LANES:kh + (k + 1) * LANES] = hi

    @pl.when(t >= na)
    def _():
        o_ref[...] = jnp.zeros(o_ref.shape, o_ref.dtype)


def _token_gather(src, tile_blk, n_active, hp, tm, d):
    s = src.shape[0]
    ks = d // 2 // LANES
    grid_spec = pltpu.PrefetchScalarGridSpec(
        num_scalar_prefetch=3,
        grid=(s // tm,),
        in_specs=[pl.BlockSpec(memory_space=pl.ANY)],
        out_specs=pl.BlockSpec((tm, d), lambda t, src_r, tb, na: (t, 0)),
        scratch_shapes=[pltpu.VMEM((2, tm * ks, LANES), jnp.uint32), pltpu.SemaphoreType.DMA((2,))])
    return pl.pallas_call(
        functools.partial(_token_gather_kernel, tm, ks),
        grid_spec=grid_spec,
        out_shape=jax.ShapeDtypeStruct((s, d), BF16),
        compiler_params=_cparams(32, 1),
        name="token_gather",
    )(src, tile_blk, n_active, hp)


def _moe_up_kernel(te_ref, tb_ref, tf_ref, na_ref, xs_ref, wg_ref, wu_ref, h_ref, wg_b, wu_b):
    t = pl.program_id(1)

    @pl.when(tf_ref[t] == 1)
    def _():
        wg_b[...] = wg_ref[...].astype(BF16)
        wu_b[...] = wu_ref[...].astype(BF16)

    @pl.when(t < na_ref[0])
    def _():
        x = xs_ref[...]
        g = jnp.dot(x, wg_b[...], preferred_element_type=F32)
        u = jnp.dot(x, wu_b[...], preferred_element_type=F32)
        h_ref[...] = (g * _sigmoid(g) * u).astype(h_ref.dtype)

    @pl.when(t >= na_ref[0])
    def _():
        h_ref[...] = jnp.zeros(h_ref.shape, h_ref.dtype)


def _moe_up(tile_expert, tile_blk, tile_first, n_active, xs, w_gate, w_up, tm):
    s, d = xs.shape
    f = w_gate.shape[3]
    tf = 512
    nt = s // tm
    grid_spec = pltpu.PrefetchScalarGridSpec(
        num_scalar_prefetch=4,
        grid=(f // tf, nt),
        in_specs=[pl.BlockSpec((tm, d), lambda c, t, te, tb, tfi, na: (tb[t], 0)),
                  pl.BlockSpec((None, None, d, tf), lambda c, t, te, tb, tfi, na: (0, te[t], 0, c)),
                  pl.BlockSpec((None, None, d, tf), lambda c, t, te, tb, tfi, na: (0, te[t], 0, c))],
        out_specs=pl.BlockSpec((tm, tf), lambda c, t, te, tb, tfi, na: (t, c)),
        scratch_shapes=[pltpu.VMEM((d, tf), BF16), pltpu.VMEM((d, tf), BF16)])
    return pl.pallas_call(
        _moe_up_kernel,
        grid_spec=grid_spec,
        out_shape=jax.ShapeDtypeStruct((s, f), BF16),
        compiler_params=_cparams(56, 2),
        name="moe_gate_up",
    )(tile_expert, tile_blk, tile_first, n_active, xs, w_gate, w_up)


def _moe_down_kernel(te_ref, tb_ref, tf_ref, na_ref, h_ref, wd_ref, y_ref, wd_b):
    t = pl.program_id(1)

    @pl.when(tf_ref[t] == 1)
    def _():
        wd_b[...] = wd_ref[...].astype(BF16)

    @pl.when(t < na_ref[0])
    def _():
        y = jnp.dot(h_ref[...], wd_b[...], preferred_element_type=F32)
        for q in range(y_ref.shape[1]):
            y_ref[:, q, :] = y[:, q * LANES:(q + 1) * LANES]

    @pl.when(t >= na_ref[0])
    def _():
        y_ref[...] = jnp.zeros(y_ref.shape, y_ref.dtype)


def _moe_down(tile_expert, tile_blk, tile_first, n_active, h, w_down, tm):
    s, f = h.shape
    d = w_down.shape[3]
    tn = 2048
    nt = s // tm
    grid_spec = pltpu.PrefetchScalarGridSpec(
        num_scalar_prefetch=4,
        grid=(d // tn, nt),
        in_specs=[pl.BlockSpec((tm, f), lambda c, t, te, tb, tfi, na: (tb[t], 0)),
                  pl.BlockSpec((None, None, f, tn), lambda c, t, te, tb, tfi, na: (0, te[t], 0, c))],
        out_specs=pl.BlockSpec((tm, tn // LANES, LANES), lambda c, t, te, tb, tfi, na: (t, c, 0)),
        scratch_shapes=[pltpu.VMEM((f, tn), BF16)])
    return pl.pallas_call(
        _moe_down_kernel,
        grid_spec=grid_spec,
        out_shape=jax.ShapeDtypeStruct((s, d // LANES, LANES), F32),
        compiler_params=_cparams(40, 2),
        name="moe_down",
    )(tile_expert, tile_blk, tile_first, n_active, h, w_down)


def _final_kernel(tm, row0, t_all, n_steps, pos_ref, y_ref, x1_ref, wt_ref, ga_ref, g_ref, o_ref, ybuf, fbuf, sem):
    i = pl.program_id(0)
    nk = y_ref.shape[1]

    def issue(step, slot):
        def body(r, carry):
            for k in range(TOP_K):
                p = pos_ref[k * t_all + row0 + step * tm + r]
                pltpu.make_async_copy(y_ref.at[p], ybuf.at[slot, k * tm + r], sem.at[slot]).start()
            return carry
        lax.fori_loop(0, tm, body, 0, unroll=8)

    @pl.when(i == 0)
    def _():
        issue(0, 0)

    @pl.when(i + 1 < n_steps)
    def _():
        issue(i + 1, (i + 1) % 2)

    slot = i % 2
    pltpu.make_async_copy(y_ref.at[pl.ds(0, TOP_K * tm)], ybuf.at[slot], sem.at[slot]).wait()
    wt = wt_ref[...]
    ss = jnp.zeros((tm, 1), F32)
    for q in range(nk):
        f = wt[:, 0:1] * ybuf[slot, pl.ds(0, tm), q, :]
        for k in range(1, TOP_K):
            f = f + wt[:, k:k + 1] * ybuf[slot, pl.ds(k * tm, tm), q, :]
        fbuf[:, q * LANES:(q + 1) * LANES] = f
        ss = ss + jnp.sum(f * f, axis=-1, keepdims=True)
    r = lax.rsqrt(ss / (nk * LANES) + NORM_EPS)
    o_ref[...] = x1_ref[...] + ga_ref[0] * (fbuf[...] * r * g_ref[...])


def _final(x1, y_slots, pos, wts, mod, g_post, row0, t_all):
    t, d = x1.shape
    tm = 128
    o0 = row0 // tm
    n_steps = t // tm
    nk = d // LANES
    grid_spec = pltpu.PrefetchScalarGridSpec(
        num_scalar_prefetch=1,
        grid=(n_steps,),
        in_specs=[pl.BlockSpec(memory_space=pl.ANY),
                  pl.BlockSpec((tm, d), lambda i, p: (i, 0)),
                  pl.BlockSpec((tm, LANES), lambda i, p: (o0 + i, 0)),
                  pl.BlockSpec((1, tm, d), lambda i, p: (0, i, 5)) if mod.per_row else
                  pl.BlockSpec((1, 1, d), lambda i, p: (i // (mod.rows_per_seq // tm), 0, 5)),
                  pl.BlockSpec((1, d), lambda i, p: (0, 0))],
        out_specs=pl.BlockSpec((tm, d), lambda i, p: (i, 0)),
        scratch_shapes=[pltpu.VMEM((2, TOP_K * tm, nk, LANES), F32), pltpu.VMEM((tm, d), F32),
                        pltpu.SemaphoreType.DMA((2,))])
    return pl.pallas_call(
        functools.partial(_final_kernel, tm, row0, t_all, n_steps),
        grid_spec=grid_spec,
        out_shape=jax.ShapeDtypeStruct((t, d), F32),
        compiler_params=_cparams(40, 1),
        name="gather_combine_residual",
    )(pos, y_slots, x1, wts, mod.arr, g_post.reshape(1, d))


def _routing_tables(eid, n_tok, tm, nt_max):
    e_flat = eid.reshape(-1)
    onehot = (e_flat[:, None] == jnp.arange(N_EXPERTS, dtype=jnp.int32)[None, :]).astype(jnp.int32)
    csum = jnp.cumsum(onehot, axis=0)
    rank = jnp.sum((csum - onehot) * onehot, axis=1)
    counts = csum[-1]
    tiles_e = (counts + tm - 1) // tm
    tile_end = jnp.cumsum(tiles_e)
    tile_start = tile_end - tiles_e
    n_active = tile_end[-1]
    slot = tile_start[e_flat] * tm + rank
    tile_ids = jnp.minimum(jnp.arange(nt_max, dtype=jnp.int32), n_active - 1)
    tile_expert = jnp.searchsorted(tile_end, tile_ids, side="right").astype(jnp.int32)
    tile_first = jnp.concatenate([jnp.ones((1,), jnp.int32),
                                  (tile_expert[1:] != tile_expert[:-1]).astype(jnp.int32)])
    tok = jnp.arange(2 * n_tok, dtype=jnp.int32) // 2
    src = jnp.zeros((nt_max * tm,), jnp.int32).at[slot].set(tok)
    pos = slot.reshape(n_tok, 2).T.reshape(-1)
    return (tile_expert, tile_ids.astype(jnp.int32), tile_first, n_active.reshape(1).astype(jnp.int32),
            src, pos.astype(jnp.int32))


def _rope_tables(pos, reps):
    half = QK_ROPE // 2
    freq = ROPE_THETA ** (-jnp.arange(half, dtype=F32) / half)
    ang = pos.astype(F32)[:, None] * freq[None, :]
    cos, sin = jnp.cos(ang), jnp.sin(ang)
    zeros = jnp.zeros((pos.shape[0], LANES - QK_ROPE), F32)
    cos_t = jnp.concatenate([cos, cos, zeros], axis=1)
    sin_t = jnp.concatenate([-sin, sin, zeros], axis=1)
    return jnp.tile(cos_t, (reps, 1)), jnp.tile(sin_t, (reps, 1))


def kernel(x_prompt, x_sample, c_prompt, c_sample, cache_kv_latent, cache_k_rope, state_conv, page_table, w_ada, b_ada, g_pre_mix, g_post_mix, g_pre_ffn, g_post_ffn, w_in, g_q_lat, g_kv_lat, w_uq, w_uk, w_uv, conv_w, conv_b, conv_ln_g, conv_ln_b, w_out, w_router_group, b_router_group, w_router_expert, b_router_expert, w_exp_gate, w_exp_up, w_exp_down):
    n_seq_p, seq_p, d = x_prompt.shape
    n_seq_s, seq_s, _ = x_sample.shape
    depth = w_ada.shape[0]
    assert depth == 1, "single-layer trunk"
    q_lora = g_q_lat.shape[1]
    kv_lora = g_kv_lat.shape[1]
    n_heads = w_uq.shape[2]
    conv_ch = conv_w.shape[2]
    past = page_table.shape[1] * cache_kv_latent.shape[2]
    t_p, t_s = n_seq_p * seq_p, n_seq_s * seq_s
    t_all = t_p + t_s
    l = 0

    w_in_l = w_in[l]
    sp = (q_lora, q_lora + kv_lora, q_lora + kv_lora + QK_ROPE, q_lora + kv_lora + QK_ROPE + conv_ch)
    w_in_b = jnp.concatenate(
        [w_in_l[:, sp[2]:sp[3]], w_in_l[:, sp[3]:], w_in_l[:, :sp[0]], w_in_l[:, sp[0]:sp[1]],
         w_in_l[:, sp[1]:sp[2]], jnp.zeros((d, LANES - QK_ROPE), F32)], axis=1).astype(BF16)
    qa_col = (2 * conv_ch) // q_lora
    kv_col = (2 * conv_ch + q_lora) // kv_lora
    kr_col = (2 * conv_ch + q_lora + kv_lora) // LANES
    wq_b = jnp.concatenate([w_uq[l], jnp.zeros((q_lora, n_heads, HEAD_PAD - QK_HEAD), F32)],
                           axis=2).reshape(q_lora, n_heads * HEAD_PAD).astype(BF16)
    wuk_b = w_uk[l].reshape(kv_lora, n_heads * QK_NOPE).astype(BF16)
    wuv_b = w_uv[l].reshape(kv_lora, n_heads * V_HEAD).astype(BF16)
    wukT_b = jnp.transpose(w_uk[l], (1, 2, 0)).astype(BF16)
    wuv_h_b = jnp.transpose(w_uv[l], (1, 0, 2)).astype(BF16)
    w_out_b = w_out[l].astype(BF16)
    wr_b = jnp.concatenate([w_router_group[l], w_router_expert[l],
                            jnp.zeros((d, LANES - N_EXPERT_GROUPS - N_EXPERTS), F32)], axis=1).astype(BF16)
    br = jnp.concatenate([b_router_group[l], b_router_expert[l],
                          jnp.zeros((LANES - N_EXPERT_GROUPS - N_EXPERTS,), F32)]).reshape(1, LANES)

    c_rows = n_seq_p + n_seq_s
    c_pad = -c_rows % 8
    c_all = jnp.concatenate([c_prompt, c_sample, jnp.zeros((c_pad, d), F32)], axis=0)
    mod_all = _ada(c_all, w_ada[l], b_ada[l])
    mod_p = _Mod(mod_all[:n_seq_p].reshape(n_seq_p, 1, 6 * d), seq_p, d)
    mod_s = _Mod(jnp.repeat(mod_all[n_seq_p:c_rows], seq_s, axis=0).reshape(1, t_s, 6 * d), seq_s, d)

    cos_p, sin_p = _rope_tables(jnp.arange(seq_p), 1)
    cos_s, sin_s = _rope_tables(past + jnp.arange(seq_s), n_seq_s)

    xp = x_prompt.reshape(t_p, d)
    xs = x_sample.reshape(t_s, d)

    z_p = _premix(xp, g_pre_mix[l], mod_p, w_in_b)
    kv_lat_p, k_rope_p, k_full_p, v_p = _kv_post(z_p, g_kv_lat[l], cos_p, sin_p, seq_p, kv_col, kr_col, wuk_b, wuv_b)
    q_full_p = _q_proj(z_p, g_q_lat[l], cos_p, sin_p, seq_p, qa_col, wq_b)
    attn_p = _flash(q_full_p.reshape(n_seq_p, seq_p, -1), k_full_p.reshape(n_seq_p, seq_p, -1),
                    v_p.reshape(n_seq_p, seq_p, -1), n_heads).reshape(t_p, n_heads * V_HEAD)
    conv_p, st_p = _conv_prompt(z_p, n_seq_p, seq_p, conv_ch, 0, conv_w[l], conv_b[l], conv_ln_g[l], conv_ln_b[l])
    x1_p, hp_p, lg_p = _outproj(attn_p, conv_p, w_out_b, xp, mod_p, g_post_mix[l], g_pre_ffn[l], wr_b, br)

    z_s = _premix(xs, g_pre_mix[l], mod_s, w_in_b)
    kv_lat_s, k_rope_s = _kv_post(z_s, g_kv_lat[l], cos_s, sin_s, t_s, kv_col, kr_col)
    q_full_s = _q_proj(z_s, g_q_lat[l], cos_s, sin_s, t_s, qa_col, wq_b)
    q_lat_s, q_rope_hs = _absorb(q_full_s, wukT_b)
    rows = seq_s * n_heads
    q_rope_s = jnp.transpose(q_rope_hs, (1, 0, 2)).reshape(n_seq_s, rows, QK_ROPE)
    o_lat_s = _paged_attn(page_table, q_lat_s.reshape(n_seq_s, rows, kv_lora), q_rope_s,
                          kv_lat_s.reshape(n_seq_s, seq_s, kv_lora), k_rope_s.reshape(n_seq_s, seq_s, QK_ROPE),
                          cache_kv_latent, jnp.swapaxes(cache_k_rope, 2, 3), n_heads)
    attn_s = _v_up(o_lat_s.reshape(t_s, n_heads * kv_lora), wuv_h_b)
    hist = state_conv.shape[2]
    state_s = state_conv.reshape(n_seq_s, hist, conv_ch)
    conv_s_t, u_s = _conv_sample(state_s, z_s.reshape(n_seq_s, seq_s, -1), conv_ch, 0,
                                 conv_w[l], conv_b[l], conv_ln_g[l], conv_ln_b[l])
    conv_s = jnp.transpose(conv_s_t, (1, 0, 2)).reshape(t_s, conv_ch)
    x1_s, hp_s, lg_s = _outproj(attn_s, conv_s, w_out_b, xs, mod_s, g_post_mix[l], g_pre_ffn[l], wr_b, br)

    hp_all = jnp.concatenate([hp_p, hp_s], axis=0)
    ids, wts = _router(jnp.concatenate([lg_p, lg_s], axis=0))
    tm_e = 256
    nt_max = (TOP_K * t_all) // tm_e + N_EXPERTS
    tile_expert, tile_blk, tile_first, n_active, src, pos = _routing_tables(ids[:, :TOP_K], t_all, tm_e, nt_max)
    xs_sorted = _token_gather(src, tile_blk, n_active, hp_all, tm_e, d)
    h_act = _moe_up(tile_expert, tile_blk, tile_first, n_active, xs_sorted, w_exp_gate, w_exp_up, tm_e)
    y_slots = _moe_down(tile_expert, tile_blk, tile_first, n_active, h_act, w_exp_down, tm_e)

    y_p = _final(x1_p, y_slots, pos, wts, mod_p, g_post_ffn[l], 0, t_all)
    y_s = _final(x1_s, y_slots, pos, wts, mod_s, g_post_ffn[l], t_p, t_all)

    conv_state_p = jnp.transpose(st_p, (0, 2, 1, 3)).reshape(n_seq_p, HALO, conv_ch)[:, HALO - (CONV_WIDTH - 1):, :]
    conv_state_s = jnp.concatenate([state_s[:, seq_s:, :], u_s], axis=1)
    return (y_p.reshape(n_seq_p, seq_p, d), y_s.reshape(n_seq_s, seq_s, d),
            kv_lat_p.reshape(1, n_seq_p, seq_p, kv_lora), k_rope_p.reshape(1, n_seq_p, seq_p, QK_ROPE),
            conv_state_p[None], kv_lat_s.reshape(1, n_seq_s, seq_s, kv_lora),
            k_rope_s.reshape(1, n_seq_s, seq_s, QK_ROPE), conv_state_s[None])
```

```python
import functools

import jax
import jax.numpy as jnp
from jax import lax
from jax.experimental import pallas as pl
from jax.experimental.pallas import tpu as pltpu

F32 = jnp.float32
BF16 = jnp.bfloat16

V_HEAD = 128
QK_NOPE = 128
QK_ROPE = 64
QK_HEAD = QK_NOPE + QK_ROPE
HEAD_PAD = 256
ROPE_THETA = 10000.0
SOFTMAX_SCALE = QK_HEAD ** -0.5
CONV_WIDTH = 31
N_EXPERT_GROUPS = 4
EXPERTS_PER_GROUP = 8
N_EXPERTS = N_EXPERT_GROUPS * EXPERTS_PER_GROUP
TOP_K = 2
NORM_EPS = 1e-6
LN_EPS = 1e-5
LANES = 128
HALO = 32

MIB = 1024 * 1024


def _cparams(vmem_mib, n_axes):
    return pltpu.CompilerParams(dimension_semantics=("arbitrary",) * n_axes,
                                vmem_limit_bytes=vmem_mib * MIB)


def _sigmoid(x):
    return 1.0 / (1.0 + jnp.exp(-x))


def _rms(x):
    return x * lax.rsqrt(jnp.mean(x * x, axis=-1, keepdims=True) + NORM_EPS)


def _rope_chunk(c, cos, sin):
    lane = lax.broadcasted_iota(jnp.int32, c.shape, 1)
    sw = jnp.where(lane < QK_ROPE // 2, pltpu.roll(c, LANES - QK_ROPE // 2, 1), pltpu.roll(c, QK_ROPE // 2, 1))
    return c * cos + sw * sin


def _ada_kernel(c_ref, w_ref, b_ref, o_ref):
    c = c_ref[...]
    a = (c * _sigmoid(c)).astype(BF16)
    o_ref[...] = jnp.dot(a, w_ref[...].astype(BF16), preferred_element_type=F32) + b_ref[...]


def _ada(c_all, w_ada, b_ada):
    m, d = c_all.shape
    n = w_ada.shape[1]
    tn = 512
    return pl.pallas_call(
        _ada_kernel,
        grid=(n // tn,),
        in_specs=[pl.BlockSpec((m, d), lambda j: (0, 0)),
                  pl.BlockSpec((d, tn), lambda j: (0, j)),
                  pl.BlockSpec((1, tn), lambda j: (0, j))],
        out_specs=pl.BlockSpec((m, tn), lambda j: (0, j)),
        out_shape=jax.ShapeDtypeStruct((m, n), F32),
        compiler_params=_cparams(40, 1),
        name="ada_ln",
    )(c_all, w_ada, b_ada.reshape(1, n))


class _Mod:
    def __init__(self, arr, rows_per_seq, d):
        self.arr = arr
        self.rows_per_seq = rows_per_seq
        self.d = d
        self.per_row = arr.shape[1] != 1

    def spec(self, tm, k, n_grid_axes):
        d = self.d
        if self.per_row:
            if n_grid_axes == 1:
                return pl.BlockSpec((1, tm, d), lambda i: (0, i, k))
            return pl.BlockSpec((1, tm, d), lambda i, j: (0, i, k))
        tiles = self.rows_per_seq // tm
        if n_grid_axes == 1:
            return pl.BlockSpec((1, 1, d), lambda i: (i // tiles, 0, k))
        return pl.BlockSpec((1, 1, d), lambda i, j: (i // tiles, 0, k))


def _premix_kernel(x_ref, g_ref, sc_ref, sh_ref, w_ref, o_ref, h_ref):
    @pl.when(pl.program_id(1) == 0)
    def _():
        h = _rms(x_ref[...]) * g_ref[...]
        h = h * (1.0 + sc_ref[0]) + sh_ref[0]
        h_ref[...] = h.astype(BF16)

    o_ref[...] = jnp.dot(h_ref[...], w_ref[...], preferred_element_type=F32)


def _premix(x, g, mod, w_b):
    t, d = x.shape
    n = w_b.shape[1]
    tm, tn = (128 if mod.per_row else 512), 640
    return pl.pallas_call(
        _premix_kernel,
        grid=(t // tm, n // tn),
        in_specs=[pl.BlockSpec((tm, d), lambda i, j: (i, 0)),
                  pl.BlockSpec((1, d), lambda i, j: (0, 0)),
                  mod.spec(tm, 1, 2), mod.spec(tm, 0, 2),
                  pl.BlockSpec((d, tn), lambda i, j: (0, j))],
        out_specs=pl.BlockSpec((tm, tn), lambda i, j: (i, j)),
        out_shape=jax.ShapeDtypeStruct((t, n), F32),
        scratch_shapes=[pltpu.VMEM((tm, d), BF16)],
        compiler_params=_cparams(48, 2),
        name="premix_in_proj",
    )(x, g.reshape(1, d), mod.arr, mod.arr, w_b)


def _kv_kernel(with_up, n_heads, kva_ref, kr_ref, g_ref, cos_ref, sin_ref, *rest):
    if with_up:
        wuk_ref, wuv_ref, kv_out, kr_out, kfull_out, v_out = rest
    else:
        kv_out, kr_out = rest
    kv = _rms(kva_ref[...]) * g_ref[...]
    kv_out[...] = kv
    rot = _rope_chunk(kr_ref[...], cos_ref[...], sin_ref[...])
    kr_out[...] = rot[:, :QK_ROPE]
    if with_up:
        kvb = kv.astype(BF16)
        k_nope = jnp.dot(kvb, wuk_ref[...], preferred_element_type=F32)
        v_out[...] = jnp.dot(kvb, wuv_ref[...], preferred_element_type=F32).astype(BF16)
        rot_b = rot.astype(BF16)
        for h in range(n_heads):
            kfull_out[:, h * HEAD_PAD:h * HEAD_PAD + QK_NOPE] = k_nope[:, h * QK_NOPE:(h + 1) * QK_NOPE].astype(BF16)
            kfull_out[:, h * HEAD_PAD + QK_NOPE:(h + 1) * HEAD_PAD] = rot_b


def _kv_post(z, g_kv, cos_t, sin_t, pos_rows, kv_col, kr_col, wuk_b=None, wuv_b=None):
    t = z.shape[0]
    kv_lora = g_kv.shape[0]
    tm = 256
    pos_tiles = pos_rows // tm
    with_up = wuk_b is not None
    in_specs = [pl.BlockSpec((tm, kv_lora), lambda i: (i, kv_col)),
                pl.BlockSpec((tm, LANES), lambda i: (i, kr_col)),
                pl.BlockSpec((1, kv_lora), lambda i: (0, 0)),
                pl.BlockSpec((tm, LANES), lambda i: (i % pos_tiles, 0)),
                pl.BlockSpec((tm, LANES), lambda i: (i % pos_tiles, 0))]
    args = [z, z, g_kv.reshape(1, kv_lora), cos_t, sin_t]
    out_specs = [pl.BlockSpec((tm, kv_lora), lambda i: (i, 0)),
                 pl.BlockSpec((tm, QK_ROPE), lambda i: (i, 0))]
    out_shape = [jax.ShapeDtypeStruct((t, kv_lora), F32), jax.ShapeDtypeStruct((t, QK_ROPE), F32)]
    n_heads = 0
    if with_up:
        n_heads = wuk_b.shape[1] // QK_NOPE
        in_specs += [pl.BlockSpec(wuk_b.shape, lambda i: (0, 0)), pl.BlockSpec(wuv_b.shape, lambda i: (0, 0))]
        args += [wuk_b, wuv_b]
        out_specs += [pl.BlockSpec((tm, n_heads * HEAD_PAD), lambda i: (i, 0)),
                      pl.BlockSpec((tm, n_heads * V_HEAD), lambda i: (i, 0))]
        out_shape += [jax.ShapeDtypeStruct((t, n_heads * HEAD_PAD), BF16),
                      jax.ShapeDtypeStruct((t, n_heads * V_HEAD), BF16)]
    return pl.pallas_call(
        functools.partial(_kv_kernel, with_up, n_heads),
        grid=(t // tm,),
        in_specs=in_specs, out_specs=out_specs, out_shape=out_shape,
        compiler_params=_cparams(32, 1),
        name="kv_post_up" if with_up else "kv_post",
    )(*args)


def _q_kernel(heads_per_step, qa_ref, g_ref, cos_ref, sin_ref, w_ref, o_ref):
    qn = (_rms(qa_ref[...]) * g_ref[...]).astype(BF16)
    q = jnp.dot(qn, w_ref[...], preferred_element_type=F32)
    cos = cos_ref[...]
    sin = sin_ref[...]
    for h in range(heads_per_step):
        lo = h * HEAD_PAD
        o_ref[:, lo:lo + QK_NOPE] = q[:, lo:lo + QK_NOPE].astype(BF16)
        o_ref[:, lo + QK_NOPE:lo + HEAD_PAD] = _rope_chunk(q[:, lo + QK_NOPE:lo + HEAD_PAD], cos, sin).astype(BF16)


def _q_proj(z, g_q, cos_t, sin_t, pos_rows, qa_col, wq_b):
    t = z.shape[0]
    q_lora, n = wq_b.shape
    tm, tn = 256, 1024
    pos_tiles = pos_rows // tm
    return pl.pallas_call(
        functools.partial(_q_kernel, tn // HEAD_PAD),
        grid=(t // tm, n // tn),
        in_specs=[pl.BlockSpec((tm, q_lora), lambda i, j: (i, qa_col)),
                  pl.BlockSpec((1, q_lora), lambda i, j: (0, 0)),
                  pl.BlockSpec((tm, LANES), lambda i, j: (i % pos_tiles, 0)),
                  pl.BlockSpec((tm, LANES), lambda i, j: (i % pos_tiles, 0)),
                  pl.BlockSpec((q_lora, tn), lambda i, j: (0, j))],
        out_specs=pl.BlockSpec((tm, tn), lambda i, j: (i, j)),
        out_shape=jax.ShapeDtypeStruct((t, n), BF16),
        compiler_params=_cparams(32, 2),
        name="q_proj_rope",
    )(z, g_q.reshape(1, q_lora), cos_t, sin_t, wq_b)


def _flash_kernel(tb, g_heads, it_ref, jt_ref, q_ref, k_ref, v_ref, o_ref, m_ref, l_ref, acc_ref):
    pair = pl.program_id(2)
    i = it_ref[pair]
    j = jt_ref[pair]

    @pl.when(j == 0)
    def _():
        m_ref[...] = jnp.full(m_ref.shape, -jnp.inf, F32)
        l_ref[...] = jnp.zeros(l_ref.shape, F32)
        acc_ref[...] = jnp.zeros(acc_ref.shape, F32)

    def step(diagonal):
        for g in range(g_heads):
            q = q_ref[:, g * HEAD_PAD:(g + 1) * HEAD_PAD]
            k = k_ref[:, g * HEAD_PAD:(g + 1) * HEAD_PAD]
            s = lax.dot_general(q, k, (((1,), (1,)), ((), ())), preferred_element_type=F32) * SOFTMAX_SCALE
            if diagonal:
                row = lax.broadcasted_iota(jnp.int32, s.shape, 0)
                col = lax.broadcasted_iota(jnp.int32, s.shape, 1)
                s = jnp.where(col <= row, s, -jnp.inf)
            m_prev = m_ref[g]
            m_new = jnp.maximum(m_prev, jnp.max(s, axis=-1, keepdims=True))
            alpha = jnp.exp(m_prev - m_new)
            p = jnp.exp(s - m_new)
            l_ref[g] = alpha * l_ref[g] + jnp.sum(p, axis=-1, keepdims=True)
            acc_ref[g] = alpha * acc_ref[g] + jnp.dot(p.astype(BF16), v_ref[:, g * V_HEAD:(g + 1) * V_HEAD],
                                                      preferred_element_type=F32)
            m_ref[g] = m_new

    @pl.when(j < i)
    def _():
        step(False)

    @pl.when(j == i)
    def _():
        step(True)
        for g in range(g_heads):
            o_ref[:, g * V_HEAD:(g + 1) * V_HEAD] = (acc_ref[g] / l_ref[g]).astype(o_ref.dtype)


def _flash(q_full, k_full, v, n_heads):
    b, t, _ = q_full.shape
    tb = 512
    g_heads = 4
    nb = t // tb
    pairs = [(i, j) for i in range(nb) for j in range(i + 1)]
    it = jnp.asarray([p[0] for p in pairs], jnp.int32)
    jt = jnp.asarray([p[1] for p in pairs], jnp.int32)
    grid_spec = pltpu.PrefetchScalarGridSpec(
        num_scalar_prefetch=2,
        grid=(b, n_heads // g_heads, len(pairs)),
        in_specs=[pl.BlockSpec((None, tb, g_heads * HEAD_PAD), lambda bb, h, p, it_r, jt_r: (bb, it_r[p], h)),
                  pl.BlockSpec((None, tb, g_heads * HEAD_PAD), lambda bb, h, p, it_r, jt_r: (bb, jt_r[p], h)),
                  pl.BlockSpec((None, tb, g_heads * V_HEAD), lambda bb, h, p, it_r, jt_r: (bb, jt_r[p], h))],
        out_specs=pl.BlockSpec((None, tb, g_heads * V_HEAD), lambda bb, h, p, it_r, jt_r: (bb, it_r[p], h)),
        scratch_shapes=[pltpu.VMEM((g_heads, tb, 1), F32), pltpu.VMEM((g_heads, tb, 1), F32),
                        pltpu.VMEM((g_heads, tb, V_HEAD), F32)])
    return pl.pallas_call(
        functools.partial(_flash_kernel, tb, g_heads),
        grid_spec=grid_spec,
        out_shape=jax.ShapeDtypeStruct((b, t, n_heads * V_HEAD), BF16),
        compiler_params=_cparams(32, 3),
        name="prompt_flash_attn",
    )(it, jt, q_full, k_full, v)


def _absorb_kernel(q_ref, w_ref, ql_ref, qr_ref):
    q = q_ref[...]
    ql_ref[...] = jnp.dot(q[:, :QK_NOPE], w_ref[...], preferred_element_type=F32).astype(BF16)
    qr_ref[...] = q[:, QK_NOPE:QK_NOPE + QK_ROPE]


def _absorb(q_full, wukT_b):
    t = q_full.shape[0]
    n_heads, _, kv_lora = wukT_b.shape
    return pl.pallas_call(
        _absorb_kernel,
        grid=(n_heads,),
        in_specs=[pl.BlockSpec((t, HEAD_PAD), lambda h: (0, h)),
                  pl.BlockSpec((None, QK_NOPE, kv_lora), lambda h: (h, 0, 0))],
        out_specs=[pl.BlockSpec((t, kv_lora), lambda h: (0, h)),
                   pl.BlockSpec((None, t, QK_ROPE), lambda h: (h, 0, 0))],
        out_shape=[jax.ShapeDtypeStruct((t, n_heads * kv_lora), BF16),
                   jax.ShapeDtypeStruct((n_heads, t, QK_ROPE), BF16)],
        compiler_params=_cparams(32, 1),
        name="sample_absorb_q",
    )(q_full, wukT_b)


def _paged_kernel(pages, page_size, n_seq, n_chunks, n_heads, pt_ref, ql_ref, qr_ref, kvn_ref, krn_ref,
                  ckv_ref, ckr_ref, o_ref, kvbuf, krbuf, kbuf, rbuf, m_ref, l_ref, acc_ref, sem):
    b = pl.program_id(0)
    c = pl.program_id(1)
    g = b * n_chunks + c

    def issue(step, slot):
        bb = step // n_chunks
        cc = step % n_chunks
        for p in range(pages):
            page = pt_ref[bb, cc * pages + p]
            pltpu.make_async_copy(ckv_ref.at[0, page], kvbuf.at[slot, p], sem.at[0, slot]).start()
            pltpu.make_async_copy(ckr_ref.at[0, page], krbuf.at[slot, p], sem.at[1, slot]).start()

    n_buf = kvbuf.shape[0]
    ahead = n_buf - 1
    n_steps = n_seq * n_chunks

    @pl.when(g == 0)
    def _():
        for s0 in range(min(ahead, n_steps)):
            issue(s0, s0)

    @pl.when(g + ahead < n_steps)
    def _():
        issue(g + ahead, (g + ahead) % n_buf)

    slot = g % n_buf
    pltpu.make_async_copy(ckv_ref.at[0, pl.ds(0, pages)], kvbuf.at[slot], sem.at[0, slot]).wait()
    pltpu.make_async_copy(ckr_ref.at[0, pl.ds(0, pages)], krbuf.at[slot], sem.at[1, slot]).wait()
    kbufs = (kbuf,)
    rbufs = (rbuf,)
    kv_refs = [kvbuf.at[slot, p] for p in range(pages)]
    kr_refs = [krbuf.at[slot, p] for p in range(pages)]

    @pl.when(c == 0)
    def _():
        m_ref[...] = jnp.full(m_ref.shape, -jnp.inf, F32)
        l_ref[...] = jnp.zeros(l_ref.shape, F32)
        acc_ref[...] = jnp.zeros(acc_ref.shape, F32)

    ql = ql_ref[...]
    qr = qr_ref[...]
    nt = (((1,), (1,)), ((), ()))
    m = m_ref[...]
    l = l_ref[...]
    acc = acc_ref[...]
    per = pages // len(kbufs)
    for sc, (kbuf, rbuf) in enumerate(zip(kbufs, rbufs)):
        for p in range(per):
            pg = sc * per + p
            kbuf[p * page_size:(p + 1) * page_size, :] = kv_refs[pg][...].astype(BF16)
            rbuf[:, p * page_size:(p + 1) * page_size] = kr_refs[pg][...].astype(BF16)
        kb = kbuf[...]
        s = (lax.dot_general(ql, kb, nt, preferred_element_type=F32)
             + jnp.dot(qr, rbuf[...], preferred_element_type=F32)) * SOFTMAX_SCALE
        m_new = jnp.maximum(m, jnp.max(s, axis=-1, keepdims=True))
        alpha = jnp.exp(m - m_new)
        p_ = jnp.exp(s - m_new)
        l = alpha * l + jnp.sum(p_, axis=-1, keepdims=True)
        acc = alpha * acc + jnp.dot(p_.astype(BF16), kb, preferred_element_type=F32)
        m = m_new
    m_ref[...] = m
    l_ref[...] = l
    acc_ref[...] = acc

    @pl.when(c == n_chunks - 1)
    def _():
        qlf = ql.astype(F32)
        qrf = qr.astype(F32)
        kvn = kvn_ref[...].astype(BF16).astype(F32)
        krn = krn_ref[...].astype(BF16).astype(F32)
        n_new = kvn.shape[0]
        tok = lax.broadcasted_iota(jnp.int32, (ql.shape[0], 1), 0) // n_heads
        s_new = []
        for jn in range(n_new):
            sj = (jnp.sum(qlf * kvn[jn:jn + 1, :], axis=-1, keepdims=True)
                  + jnp.sum(qrf * krn[jn:jn + 1, :], axis=-1, keepdims=True)) * SOFTMAX_SCALE
            s_new.append(jnp.where(tok >= jn, sj, -jnp.inf))
        m_prev2 = m_ref[...]
        m_fin = m_prev2
        for sj in s_new:
            m_fin = jnp.maximum(m_fin, sj)
        alpha2 = jnp.exp(m_prev2 - m_fin)
        l_fin = alpha2 * l_ref[...]
        acc = alpha2 * acc_ref[...]
        for jn, sj in enumerate(s_new):
            pj = jnp.exp(sj - m_fin)
            l_fin = l_fin + pj
            acc = acc + pj.astype(BF16).astype(F32) * kvn[jn:jn + 1, :]
        o_ref[...] = (acc / l_fin).astype(o_ref.dtype)


def _paged_attn(page_table, q_lat, q_rope, kv_new, kr_new, cache_kv, cache_kr_t, n_heads):
    n_seq, rows, kv_lora = q_lat.shape
    page_size = cache_kv.shape[2]
    n_pages = page_table.shape[1]
    n_new = kv_new.shape[1]
    pages = 32
    n_buf = 3
    n_chunks = n_pages // pages
    in_specs = [pl.BlockSpec((None, rows, kv_lora), lambda b, c, pt: (b, 0, 0)),
                pl.BlockSpec((None, rows, QK_ROPE), lambda b, c, pt: (b, 0, 0)),
                pl.BlockSpec((None, n_new, kv_lora), lambda b, c, pt: (b, 0, 0)),
                pl.BlockSpec((None, n_new, QK_ROPE), lambda b, c, pt: (b, 0, 0)),
                pl.BlockSpec(memory_space=pl.ANY),
                pl.BlockSpec(memory_space=pl.ANY)]
    grid_spec = pltpu.PrefetchScalarGridSpec(
        num_scalar_prefetch=1,
        grid=(n_seq, n_chunks),
        in_specs=in_specs,
        out_specs=pl.BlockSpec((None, rows, kv_lora), lambda b, c, pt: (b, 0, 0)),
        scratch_shapes=[pltpu.VMEM((n_buf, pages, page_size, kv_lora), F32),
                        pltpu.VMEM((n_buf, pages, QK_ROPE, page_size), F32),
                        pltpu.VMEM((pages * page_size, kv_lora), BF16),
                        pltpu.VMEM((QK_ROPE, pages * page_size), BF16),
                        pltpu.VMEM((rows, 1), F32), pltpu.VMEM((rows, 1), F32),
                        pltpu.VMEM((rows, kv_lora), F32),
                        pltpu.SemaphoreType.DMA((2, n_buf))])
    return pl.pallas_call(
        functools.partial(_paged_kernel, pages, page_size, n_seq, n_chunks, n_heads),
        grid_spec=grid_spec,
        out_shape=jax.ShapeDtypeStruct((n_seq, rows, kv_lora), BF16),
        compiler_params=_cparams(48, 2),
        name="sample_paged_attn",
    )(page_table, q_lat, q_rope, kv_new, kr_new, cache_kv, cache_kr_t)


def _vup_kernel(o_ref, w_ref, a_ref):
    a_ref[...] = jnp.dot(o_ref[...], w_ref[...], preferred_element_type=F32).astype(BF16)


def _v_up(o_lat, wuv_h_b):
    t = o_lat.shape[0]
    n_heads, kv_lora, _ = wuv_h_b.shape
    return pl.pallas_call(
        _vup_kernel,
        grid=(n_heads,),
        in_specs=[pl.BlockSpec((t, kv_lora), lambda h: (0, h)),
                  pl.BlockSpec((None, kv_lora, V_HEAD), lambda h: (h, 0, 0))],
        out_specs=pl.BlockSpec((t, V_HEAD), lambda h: (0, h)),
        out_shape=jax.ShapeDtypeStruct((t, n_heads * V_HEAD), BF16),
        compiler_params=_cparams(32, 1),
        name="sample_v_up",
    )(o_lat, wuv_h_b)


def _ln_swish(y, g, b):
    mu = jnp.mean(y, axis=-1, keepdims=True)
    yc = y - mu
    var = jnp.mean(yc * yc, axis=-1, keepdims=True)
    o = yc * lax.rsqrt(var + LN_EPS) * g + b
    return o * _sigmoid(o)


def _conv_prompt_kernel(tt, cw, n_cc, val_ref, gate_ref, hval_ref, hgate_ref, w_ref, b_ref, g_ref, lb_ref,
                        o_ref, st_ref, ext_ref, y_ref):
    i = pl.program_id(1)
    cc = pl.program_id(2)
    um = val_ref[...] * _sigmoid(gate_ref[...])
    uh = hval_ref[...] * _sigmoid(hgate_ref[...])
    uh = jnp.where(i == 0, 0.0, uh)
    ext_ref[0:HALO, :] = uh
    ext_ref[HALO:HALO + tt, :] = um
    st_ref[cc] = um[tt - HALO:, :]
    rb = 64
    w = w_ref[...]
    bias = b_ref[...]
    for r in range(0, tt, rb):
        acc = jnp.zeros((rb, cw), F32) + bias
        for k in range(CONV_WIDTH):
            s0 = r + k + HALO - (CONV_WIDTH - 1)
            acc = acc + w[k:k + 1, :] * ext_ref[s0:s0 + rb, :]
        y_ref[cc, r:r + rb, :] = acc

    @pl.when(cc == n_cc - 1)
    def _():
        s1 = jnp.zeros((tt, 1), F32)
        for c2 in range(n_cc):
            s1 = s1 + jnp.sum(y_ref[c2], axis=-1, keepdims=True)
        mu = s1 / (n_cc * cw)
        s2 = jnp.zeros((tt, 1), F32)
        for c2 in range(n_cc):
            d = y_ref[c2] - mu
            s2 = s2 + jnp.sum(d * d, axis=-1, keepdims=True)
        rs = lax.rsqrt(s2 / (n_cc * cw) + LN_EPS)
        for c2 in range(n_cc):
            o = (y_ref[c2] - mu) * rs * g_ref[:, c2 * cw:(c2 + 1) * cw] + lb_ref[:, c2 * cw:(c2 + 1) * cw]
            o_ref[:, c2 * cw:(c2 + 1) * cw] = (o * _sigmoid(o)).astype(o_ref.dtype)


def _conv_prompt(z, n_seq, seq, conv_ch, val_col0, conv_w, conv_b, ln_g, ln_b):
    tt, cw = 256, 256
    n_cc = conv_ch // cw
    nt = seq // tt
    hb = tt // HALO
    gate_off = conv_ch // cw

    def main(off):
        return pl.BlockSpec((tt, cw), lambda b, i, c: (b * nt + i, val_col0 + off + c))

    def halo(off):
        return pl.BlockSpec((HALO, cw), lambda b, i, c: (jnp.maximum((b * nt + i) * hb - 1, 0), val_col0 + off + c))

    return pl.pallas_call(
        functools.partial(_conv_prompt_kernel, tt, cw, n_cc),
        grid=(n_seq, nt, n_cc),
        in_specs=[main(0), main(gate_off), halo(0), halo(gate_off),
                  pl.BlockSpec((CONV_WIDTH, cw), lambda b, i, c: (0, c)),
                  pl.BlockSpec((1, cw), lambda b, i, c: (0, c)),
                  pl.BlockSpec((1, conv_ch), lambda b, i, c: (0, 0)),
                  pl.BlockSpec((1, conv_ch), lambda b, i, c: (0, 0))],
        out_specs=[pl.BlockSpec((tt, conv_ch), lambda b, i, c: (b * nt + i, 0)),
                   pl.BlockSpec((None, n_cc, HALO, cw), lambda b, i, c: (b, 0, 0, 0))],
        out_shape=[jax.ShapeDtypeStruct((n_seq * seq, conv_ch), BF16),
                   jax.ShapeDtypeStruct((n_seq, n_cc, HALO, cw), F32)],
        scratch_shapes=[pltpu.VMEM((HALO + tt, cw), F32), pltpu.VMEM((n_cc, tt, cw), F32)],
        compiler_params=_cparams(32, 3),
        name="conv_prompt",
    )(z, z, z, z, conv_w, conv_b.reshape(1, conv_ch), ln_g.reshape(1, conv_ch), ln_b.reshape(1, conv_ch))


def _conv_sample_kernel(n_new, st_ref, val_ref, gate_ref, wst_ref, wu_ref, b_ref, g_ref, lb_ref, o_ref, u_ref):
    u = val_ref[...] * _sigmoid(gate_ref[...])
    u_ref[...] = u
    st = st_ref[...]
    for t in range(n_new):
        y = (jnp.sum(st * wst_ref[t][None], axis=1) + jnp.sum(u * wu_ref[t][None], axis=1) + b_ref[...])
        o_ref[t] = _ln_swish(y, g_ref[...], lb_ref[...]).astype(o_ref.dtype)


def _conv_sample(state, z3, conv_ch, val_col0, conv_w, conv_b, ln_g, ln_b):
    n_seq, hist, _ = state.shape
    n_new = z3.shape[1]
    nb = 16
    jj = jnp.arange(hist)[None, :] - jnp.arange(n_new)[:, None]
    wst = jnp.where((jj >= 0)[..., None], conv_w[jnp.clip(jj, 0, CONV_WIDTH - 1)], 0.0)
    ii = hist - jnp.arange(n_new)[:, None] + jnp.arange(n_new)[None, :]
    wu = jnp.where((ii <= CONV_WIDTH - 1)[..., None], conv_w[jnp.clip(ii, 0, CONV_WIDTH - 1)], 0.0)
    gate_off = 1
    return pl.pallas_call(
        functools.partial(_conv_sample_kernel, n_new),
        grid=(n_seq // nb,),
        in_specs=[pl.BlockSpec((nb, hist, conv_ch), lambda i: (i, 0, 0)),
                  pl.BlockSpec((nb, n_new, conv_ch), lambda i: (i, 0, val_col0)),
                  pl.BlockSpec((nb, n_new, conv_ch), lambda i: (i, 0, val_col0 + gate_off)),
                  pl.BlockSpec((n_new, hist, conv_ch), lambda i: (0, 0, 0)),
                  pl.BlockSpec((n_new, n_new, conv_ch), lambda i: (0, 0, 0)),
                  pl.BlockSpec((1, conv_ch), lambda i: (0, 0)),
                  pl.BlockSpec((1, conv_ch), lambda i: (0, 0)),
                  pl.BlockSpec((1, conv_ch), lambda i: (0, 0))],
        out_specs=[pl.BlockSpec((n_new, nb, conv_ch), lambda i: (0, i, 0)),
                   pl.BlockSpec((nb, n_new, conv_ch), lambda i: (i, 0, 0))],
        out_shape=[jax.ShapeDtypeStruct((n_new, n_seq, conv_ch), BF16),
                   jax.ShapeDtypeStruct((n_seq, n_new, conv_ch), F32)],
        compiler_params=_cparams(32, 1),
        name="conv_sample",
    )(state, z3, z3, wst, wu, conv_b.reshape(1, conv_ch), ln_g.reshape(1, conv_ch), ln_b.reshape(1, conv_ch))


def _pack_pair(lo, hi):
    lo_u = lax.bitcast_convert_type(lo.astype(BF16).astype(F32), jnp.uint32)
    hi_u = lax.bitcast_convert_type(hi.astype(BF16).astype(F32), jnp.uint32)
    return (lo_u >> 16) | (hi_u & jnp.uint32(0xFFFF0000))


def _unpack_pair(w):
    lo = lax.bitcast_convert_type(w << 16, F32).astype(BF16)
    hi = lax.bitcast_convert_type(w & jnp.uint32(0xFFFF0000), F32).astype(BF16)
    return lo, hi


def _outproj_kernel(nj, tn, a_ref, c_ref, wt_ref, wb_ref, x_ref, ga_ref, gpost_ref, gpre_ref, sc_ref, sh_ref,
                    wr_ref, br_ref, x1_ref, hp_ref, lg_ref, mix_ref):
    j = pl.program_id(1)
    mix_ref[j] = (jnp.dot(a_ref[...], wt_ref[...], preferred_element_type=F32)
                  + jnp.dot(c_ref[...], wb_ref[...], preferred_element_type=F32))

    @pl.when(j == nj - 1)
    def _():
        tm = x_ref.shape[0]
        d = nj * tn
        ss = jnp.zeros((tm, 1), F32)
        for c in range(nj):
            m = mix_ref[c]
            ss = ss + jnp.sum(m * m, axis=-1, keepdims=True)
        r = lax.rsqrt(ss / d + NORM_EPS)
        ss1 = jnp.zeros((tm, 1), F32)
        for c in range(nj):
            cs = slice(c * tn, (c + 1) * tn)
            x1 = x_ref[:, cs] + ga_ref[0, :, cs] * (mix_ref[c] * r * gpost_ref[:, cs])
            x1_ref[:, cs] = x1
            ss1 = ss1 + jnp.sum(x1 * x1, axis=-1, keepdims=True)
        r1 = lax.rsqrt(ss1 / d + NORM_EPS)
        half = nj // 2
        lg = jnp.zeros(lg_ref.shape, F32) + br_ref[...]
        for c in range(half):
            parts = []
            for cc in (c, c + half):
                cs = slice(cc * tn, (cc + 1) * tn)
                h = (x1_ref[:, cs] * r1 * gpre_ref[:, cs]) * (1.0 + sc_ref[0, :, cs]) + sh_ref[0, :, cs]
                lg = lg + jnp.dot(h.astype(BF16), wr_ref[cs, :], preferred_element_type=F32)
                parts.append(h)
            packed = _pack_pair(parts[0], parts[1])
            ks = d // 2 // LANES
            for q in range(tn // LANES):
                k = c * (tn // LANES) + q
                hp_ref[pl.ds(k, tm, stride=ks), :] = packed[:, q * LANES:(q + 1) * LANES]
        lg_ref[...] = lg


def _outproj(attn, conv, w_out_b, x, mod, g_post, g_pre, wr_b, br):
    t, d = x.shape
    kh = attn.shape[1]
    tm, tn = (128 if mod.per_row else 256), 512
    nj = d // tn
    ks = d // 2 // LANES
    return pl.pallas_call(
        functools.partial(_outproj_kernel, nj, tn),
        grid=(t // tm, nj),
        in_specs=[pl.BlockSpec((tm, kh), lambda i, j: (i, 0)),
                  pl.BlockSpec((tm, kh), lambda i, j: (i, 0)),
                  pl.BlockSpec((kh, tn), lambda i, j: (0, j)),
                  pl.BlockSpec((kh, tn), lambda i, j: (1, j)),
                  pl.BlockSpec((tm, d), lambda i, j: (i, 0)),
                  mod.spec(tm, 2, 2),
                  pl.BlockSpec((1, d), lambda i, j: (0, 0)),
                  pl.BlockSpec((1, d), lambda i, j: (0, 0)),
                  mod.spec(tm, 4, 2), mod.spec(tm, 3, 2),
                  pl.BlockSpec((d, LANES), lambda i, j: (0, 0)),
                  pl.BlockSpec((1, LANES), lambda i, j: (0, 0))],
        out_specs=[pl.BlockSpec((tm, d), lambda i, j: (i, 0)),
                   pl.BlockSpec((tm * ks, LANES), lambda i, j: (i, 0)),
                   pl.BlockSpec((tm, LANES), lambda i, j: (i, 0))],
        out_shape=[jax.ShapeDtypeStruct((t, d), F32),
                   jax.ShapeDtypeStruct((t * ks, LANES), jnp.uint32),
                   jax.ShapeDtypeStruct((t, LANES), F32)],
        scratch_shapes=[pltpu.VMEM((nj, tm, tn), F32)],
        compiler_params=_cparams(48, 2),
        name="out_proj_residual_prenorm",
    )(attn, conv, w_out_b, w_out_b, x, mod.arr, g_post.reshape(1, d), g_pre.reshape(1, d), mod.arr, mod.arr,
      wr_b, br)


def _router_kernel(lg_ref, id_ref, wt_ref):
    lg = lg_ref[...]
    lane = lax.broadcasted_iota(jnp.int32, lg.shape, 1)
    is_g = lane < N_EXPERT_GROUPS
    gmax = jnp.max(jnp.where(is_g, lg, -jnp.inf), axis=-1, keepdims=True)
    gexp = jnp.where(is_g, jnp.exp(jnp.where(is_g, lg, gmax) - gmax), 0.0)
    gprob = gexp / jnp.sum(gexp, axis=-1, keepdims=True)
    gw = jnp.max(gprob, axis=-1, keepdims=True)
    gidx = jnp.min(jnp.where(is_g & (gprob == gw), lane, LANES), axis=-1, keepdims=True)
    lo = N_EXPERT_GROUPS + gidx * EXPERTS_PER_GROUP
    is_e = (lane >= lo) & (lane < lo + EXPERTS_PER_GROUP)
    emax = jnp.max(jnp.where(is_e, lg, -jnp.inf), axis=-1, keepdims=True)
    eexp = jnp.where(is_e, jnp.exp(jnp.where(is_e, lg, emax) - emax), 0.0)
    ep = jnp.where(is_e, eexp / jnp.sum(eexp, axis=-1, keepdims=True), -1.0)
    p1 = jnp.max(ep, axis=-1, keepdims=True)
    i1 = jnp.min(jnp.where(ep == p1, lane, LANES), axis=-1, keepdims=True)
    ep2 = jnp.where(lane == i1, -1.0, ep)
    p2 = jnp.max(ep2, axis=-1, keepdims=True)
    i2 = jnp.min(jnp.where(ep2 == p2, lane, LANES), axis=-1, keepdims=True)
    den = p1 + p2
    w1 = p1 / den * gw
    w2 = p2 / den * gw
    id_ref[...] = jnp.where(lane == 0, i1 - N_EXPERT_GROUPS, jnp.where(lane == 1, i2 - N_EXPERT_GROUPS, 0))
    wt_ref[...] = jnp.where(lane == 0, w1, jnp.where(lane == 1, w2, 0.0))


def _router(logits):
    t = logits.shape[0]
    tm = 512
    spec = pl.BlockSpec((tm, LANES), lambda i: (i, 0))
    return pl.pallas_call(
        _router_kernel,
        grid=(t // tm,),
        in_specs=[spec], out_specs=[spec, spec],
        out_shape=[jax.ShapeDtypeStruct((t, LANES), jnp.int32), jax.ShapeDtypeStruct((t, LANES), F32)],
        compiler_params=_cparams(32, 1),
        name="router_topk",
    )(logits)


def _token_gather_kernel(tm, ks, src_ref, tb_ref, na_ref, hp_ref, o_ref, buf, sem):
    t = pl.program_id(0)
    na = na_ref[0]

    def issue(tile, slot):
        def body(r, carry):
            tok = src_ref[tile * tm + r]
            pltpu.make_async_copy(hp_ref.at[pl.ds(pl.multiple_of(tok * ks, ks), ks)],
                                  buf.at[slot, pl.ds(pl.multiple_of(r * ks, ks), ks)],
                                  sem.at[slot]).start()
            return carry
        lax.fori_loop(0, tm, body, 0, unroll=8)

    @pl.when(t == 0)
    def _():
        issue(0, 0)

    @pl.when(t + 1 < na)
    def _():
        issue(t + 1, (t + 1) % 2)

    @pl.when(t < na)
    def _():
        slot = t % 2
        pltpu.make_async_copy(hp_ref.at[pl.ds(0, tm * ks)], buf.at[slot], sem.at[slot]).wait()
        kh = ks * LANES
        for k in range(ks):
            lo, hi = _unpack_pair(buf[slot, pl.ds(k, tm, stride=ks), :])
            o_ref[:, k * LANES:(k + 1) * LANES] = lo
            o_ref[:, kh + k * LANES:kh + (k + 1) * LANES] = hi

    @pl.when(t >= na)
    def _():
        o_ref[...] = jnp.zeros(o_ref.shape, o_ref.dtype)


def _token_gather(src, tile_blk, n_active, hp, tm, d):
    s = src.shape[0]
    ks = d // 2 // LANES
    grid_spec = pltpu.PrefetchScalarGridSpec(
        num_scalar_prefetch=3,
        grid=(s // tm,),
        in_specs=[pl.BlockSpec(memory_space=pl.ANY)],
        out_specs=pl.BlockSpec((tm, d), lambda t, src_r, tb, na: (t, 0)),
        scratch_shapes=[pltpu.VMEM((2, tm * ks, LANES), jnp.uint32), pltpu.SemaphoreType.DMA((2,))])
    return pl.pallas_call(
        functools.partial(_token_gather_kernel, tm, ks),
        grid_spec=grid_spec,
        out_shape=jax.ShapeDtypeStruct((s, d), BF16),
        compiler_params=_cparams(32, 1),
        name="token_gather",
    )(src, tile_blk, n_active, hp)


def _moe_up_kernel(te_ref, tb_ref, tf_ref, na_ref, xs_ref, wg_ref, wu_ref, h_ref, wg_b, wu_b):
    t = pl.program_id(1)

    @pl.when(tf_ref[t] == 1)
    def _():
        wg_b[...] = wg_ref[...].astype(BF16)
        wu_b[...] = wu_ref[...].astype(BF16)

    @pl.when(t < na_ref[0])
    def _():
        x = xs_ref[...]
        g = jnp.dot(x, wg_b[...], preferred_element_type=F32)
        u = jnp.dot(x, wu_b[...], preferred_element_type=F32)
        h_ref[...] = (g * _sigmoid(g) * u).astype(h_ref.dtype)

    @pl.when(t >= na_ref[0])
    def _():
        h_ref[...] = jnp.zeros(h_ref.shape, h_ref.dtype)


def _moe_up(tile_expert, tile_blk, tile_first, n_active, xs, w_gate, w_up, tm):
    s, d = xs.shape
    f = w_gate.shape[3]
    tf = 512
    nt = s // tm
    grid_spec = pltpu.PrefetchScalarGridSpec(
        num_scalar_prefetch=4,
        grid=(f // tf, nt),
        in_specs=[pl.BlockSpec((tm, d), lambda c, t, te, tb, tfi, na: (tb[t], 0)),
                  pl.BlockSpec((None, None, d, tf), lambda c, t, te, tb, tfi, na: (0, te[t], 0, c)),
                  pl.BlockSpec((None, None, d, tf), lambda c, t, te, tb, tfi, na: (0, te[t], 0, c))],
        out_specs=pl.BlockSpec((tm, tf), lambda c, t, te, tb, tfi, na: (t, c)),
        scratch_shapes=[pltpu.VMEM((d, tf), BF16), pltpu.VMEM((d, tf), BF16)])
    return pl.pallas_call(
        _moe_up_kernel,
        grid_spec=grid_spec,
        out_shape=jax.ShapeDtypeStruct((s, f), BF16),
        compiler_params=_cparams(56, 2),
        name="moe_gate_up",
    )(tile_expert, tile_blk, tile_first, n_active, xs, w_gate, w_up)


def _moe_down_kernel(te_ref, tb_ref, tf_ref, na_ref, h_ref, wd_ref, y_ref, wd_b):
    t = pl.program_id(1)

    @pl.when(tf_ref[t] == 1)
    def _():
        wd_b[...] = wd_ref[...].astype(BF16)

    @pl.when(t < na_ref[0])
    def _():
        y = jnp.dot(h_ref[...], wd_b[...], preferred_element_type=F32)
        for q in range(y_ref.shape[1]):
            y_ref[:, q, :] = y[:, q * LANES:(q + 1) * LANES]

    @pl.when(t >= na_ref[0])
    def _():
        y_ref[...] = jnp.zeros(y_ref.shape, y_ref.dtype)


def _moe_down(tile_expert, tile_blk, tile_first, n_active, h, w_down, tm):
    s, f = h.shape
    d = w_down.shape[3]
    tn = 2048
    nt = s // tm
    grid_spec = pltpu.PrefetchScalarGridSpec(
        num_scalar_prefetch=4,
        grid=(d // tn, nt),
        in_specs=[pl.BlockSpec((tm, f), lambda c, t, te, tb, tfi, na: (tb[t], 0)),
                  pl.BlockSpec((None, None, f, tn), lambda c, t, te, tb, tfi, na: (0, te[t], 0, c))],
        out_specs=pl.BlockSpec((tm, tn // LANES, LANES), lambda c, t, te, tb, tfi, na: (t, c, 0)),
        scratch_shapes=[pltpu.VMEM((f, tn), BF16)])
    return pl.pallas_call(
        _moe_down_kernel,
        grid_spec=grid_spec,
        out_shape=jax.ShapeDtypeStruct((s, d // LANES, LANES), F32),
        compiler_params=_cparams(40, 2),
        name="moe_down",
    )(tile_expert, tile_blk, tile_first, n_active, h, w_down)


def _final_kernel(tm, row0, t_all, n_steps, pos_ref, y_ref, x1_ref, wt_ref, ga_ref, g_ref, o_ref, ybuf, fbuf, sem):
    i = pl.program_id(0)
    nk = y_ref.shape[1]

    def issue(step, slot):
        def body(r, carry):
            for k in range(TOP_K):
                p = pos_ref[k * t_all + row0 + step * tm + r]
                pltpu.make_async_copy(y_ref.at[p], ybuf.at[slot, k * tm + r], sem.at[slot]).start()
            return carry
        lax.fori_loop(0, tm, body, 0, unroll=8)

    @pl.when(i == 0)
    def _():
        issue(0, 0)

    @pl.when(i + 1 < n_steps)
    def _():
        issue(i + 1, (i + 1) % 2)

    slot = i % 2
    pltpu.make_async_copy(y_ref.at[pl.ds(0, TOP_K * tm)], ybuf.at[slot], sem.at[slot]).wait()
    wt = wt_ref[...]
    ss = jnp.zeros((tm, 1), F32)
    for q in range(nk):
        f = wt[:, 0:1] * ybuf[slot, pl.ds(0, tm), q, :]
        for k in range(1, TOP_K):
            f = f + wt[:, k:k + 1] * ybuf[slot, pl.ds(k * tm, tm), q, :]
        fbuf[:, q * LANES:(q + 1) * LANES] = f
        ss = ss + jnp.sum(f * f, axis=-1, keepdims=True)
    r = lax.rsqrt(ss / (nk * LANES) + NORM_EPS)
    o_ref[...] = x1_ref[...] + ga_ref[0] * (fbuf[...] * r * g_ref[...])


def _final(x1, y_slots, pos, wts, mod, g_post, row0, t_all):
    t, d = x1.shape
    tm = 128
    o0 = row0 // tm
    n_steps = t // tm
    nk = d // LANES
    grid_spec = pltpu.PrefetchScalarGridSpec(
        num_scalar_prefetch=1,
        grid=(n_steps,),
        in_specs=[pl.BlockSpec(memory_space=pl.ANY),
                  pl.BlockSpec((tm, d), lambda i, p: (i, 0)),
                  pl.BlockSpec((tm, LANES), lambda i, p: (o0 + i, 0)),
                  pl.BlockSpec((1, tm, d), lambda i, p: (0, i, 5)) if mod.per_row else
                  pl.BlockSpec((1, 1, d), lambda i, p: (i // (mod.rows_per_seq // tm), 0, 5)),
                  pl.BlockSpec((1, d), lambda i, p: (0, 0))],
        out_specs=pl.BlockSpec((tm, d), lambda i, p: (i, 0)),
        scratch_shapes=[pltpu.VMEM((2, TOP_K * tm, nk, LANES), F32), pltpu.VMEM((tm, d), F32),
                        pltpu.SemaphoreType.DMA((2,))])
    return pl.pallas_call(
        functools.partial(_final_kernel, tm, row0, t_all, n_steps),
        grid_spec=grid_spec,
        out_shape=jax.ShapeDtypeStruct((t, d), F32),
        compiler_params=_cparams(40, 1),
        name="gather_combine_residual",
    )(pos, y_slots, x1, wts, mod.arr, g_post.reshape(1, d))


def _routing_tables(eid, n_tok, tm, nt_max):
    e_flat = eid.reshape(-1)
    onehot = (e_flat[:, None] == jnp.arange(N_EXPERTS, dtype=jnp.int32)[None, :]).astype(jnp.int32)
    csum = jnp.cumsum(onehot, axis=0)
    rank = jnp.sum((csum - onehot) * onehot, axis=1)
    counts = csum[-1]
    tiles_e = (counts + tm - 1) // tm
    tile_end = jnp.cumsum(tiles_e)
    tile_start = tile_end - tiles_e
    n_active = tile_end[-1]
    slot = tile_start[e_flat] * tm + rank
    tile_ids = jnp.minimum(jnp.arange(nt_max, dtype=jnp.int32), n_active - 1)
    tile_expert = jnp.searchsorted(tile_end, tile_ids, side="right").astype(jnp.int32)
    tile_first = jnp.concatenate([jnp.ones((1,), jnp.int32),
                                  (tile_expert[1:] != tile_expert[:-1]).astype(jnp.int32)])
    tok = jnp.arange(2 * n_tok, dtype=jnp.int32) // 2
    src = jnp.zeros((nt_max * tm,), jnp.int32).at[slot].set(tok)
    pos = slot.reshape(n_tok, 2).T.reshape(-1)
    return (tile_expert, tile_ids.astype(jnp.int32), tile_first, n_active.reshape(1).astype(jnp.int32),
            src, pos.astype(jnp.int32))


def _rope_tables(pos, reps):
    half = QK_ROPE // 2
    freq = ROPE_THETA ** (-jnp.arange(half, dtype=F32) / half)
    ang = pos.astype(F32)[:, None] * freq[None, :]
    cos, sin = jnp.cos(ang), jnp.sin(ang)
    zeros = jnp.zeros((pos.shape[0], LANES - QK_ROPE), F32)
    cos_t = jnp.concatenate([cos, cos, zeros], axis=1)
    sin_t = jnp.concatenate([-sin, sin, zeros], axis=1)
    return jnp.tile(cos_t, (reps, 1)), jnp.tile(sin_t, (reps, 1))


def kernel(x_prompt, x_sample, c_prompt, c_sample, cache_kv_latent, cache_k_rope, state_conv, page_table, w_ada, b_ada, g_pre_mix, g_post_mix, g_pre_ffn, g_post_ffn, w_in, g_q_lat, g_kv_lat, w_uq, w_uk, w_uv, conv_w, conv_b, conv_ln_g, conv_ln_b, w_out, w_router_group, b_router_group, w_router_expert, b_router_expert, w_exp_gate, w_exp_up, w_exp_down):
    n_seq_p, seq_p, d = x_prompt.shape
    n_seq_s, seq_s, _ = x_sample.shape
    depth = w_ada.shape[0]
    assert depth == 1, "single-layer trunk"
    q_lora = g_q_lat.shape[1]
    kv_lora = g_kv_lat.shape[1]
    n_heads = w_uq.shape[2]
    conv_ch = conv_w.shape[2]
    past = page_table.shape[1] * cache_kv_latent.shape[2]
    t_p, t_s = n_seq_p * seq_p, n_seq_s * seq_s
    t_all = t_p + t_s
    l = 0

    w_in_l = w_in[l]
    sp = (q_lora, q_lora + kv_lora, q_lora + kv_lora + QK_ROPE, q_lora + kv_lora + QK_ROPE + conv_ch)
    w_in_b = jnp.concatenate(
        [w_in_l[:, sp[2]:sp[3]], w_in_l[:, sp[3]:], w_in_l[:, :sp[0]], w_in_l[:, sp[0]:sp[1]],
         w_in_l[:, sp[1]:sp[2]], jnp.zeros((d, LANES - QK_ROPE), F32)], axis=1).astype(BF16)
    qa_col = (2 * conv_ch) // q_lora
    kv_col = (2 * conv_ch + q_lora) // kv_lora
    kr_col = (2 * conv_ch + q_lora + kv_lora) // LANES
    wq_b = jnp.concatenate([w_uq[l], jnp.zeros((q_lora, n_heads, HEAD_PAD - QK_HEAD), F32)],
                           axis=2).reshape(q_lora, n_heads * HEAD_PAD).astype(BF16)
    wuk_b = w_uk[l].reshape(kv_lora, n_heads * QK_NOPE).astype(BF16)
    wuv_b = w_uv[l].reshape(kv_lora, n_heads * V_HEAD).astype(BF16)
    wukT_b = jnp.transpose(w_uk[l], (1, 2, 0)).astype(BF16)
    wuv_h_b = jnp.transpose(w_uv[l], (1, 0, 2)).astype(BF16)
    w_out_b = w_out[l].astype(BF16)
    wr_b = jnp.concatenate([w_router_group[l], w_router_expert[l],
                            jnp.zeros((d, LANES - N_EXPERT_GROUPS - N_EXPERTS), F32)], axis=1).astype(BF16)
    br = jnp.concatenate([b_router_group[l], b_router_expert[l],
                          jnp.zeros((LANES - N_EXPERT_GROUPS - N_EXPERTS,), F32)]).reshape(1, LANES)

    c_rows = n_seq_p + n_seq_s
    c_pad = -c_rows % 8
    c_all = jnp.concatenate([c_prompt, c_sample, jnp.zeros((c_pad, d), F32)], axis=0)
    mod_all = _ada(c_all, w_ada[l], b_ada[l])
    mod_p = _Mod(mod_all[:n_seq_p].reshape(n_seq_p, 1, 6 * d), seq_p, d)
    mod_s = _Mod(jnp.repeat(mod_all[n_seq_p:c_rows], seq_s, axis=0).reshape(1, t_s, 6 * d), seq_s, d)

    cos_p, sin_p = _rope_tables(jnp.arange(seq_p), 1)
    cos_s, sin_s = _rope_tables(past + jnp.arange(seq_s), n_seq_s)

    xp = x_prompt.reshape(t_p, d)
    xs = x_sample.reshape(t_s, d)

    z_p = _premix(xp, g_pre_mix[l], mod_p, w_in_b)
    kv_lat_p, k_rope_p, k_full_p, v_p = _kv_post(z_p, g_kv_lat[l], cos_p, sin_p, seq_p, kv_col, kr_col, wuk_b, wuv_b)
    q_full_p = _q_proj(z_p, g_q_lat[l], cos_p, sin_p, seq_p, qa_col, wq_b)
    attn_p = _flash(q_full_p.reshape(n_seq_p, seq_p, -1), k_full_p.reshape(n_seq_p, seq_p, -1),
                    v_p.reshape(n_seq_p, seq_p, -1), n_heads).reshape(t_p, n_heads * V_HEAD)
    conv_p, st_p = _conv_prompt(z_p, n_seq_p, seq_p, conv_ch, 0, conv_w[l], conv_b[l], conv_ln_g[l], conv_ln_b[l])
    x1_p, hp_p, lg_p = _outproj(attn_p, conv_p, w_out_b, xp, mod_p, g_post_mix[l], g_pre_ffn[l], wr_b, br)

    z_s = _premix(xs, g_pre_mix[l], mod_s, w_in_b)
    kv_lat_s, k_rope_s = _kv_post(z_s, g_kv_lat[l], cos_s, sin_s, t_s, kv_col, kr_col)
    q_full_s = _q_proj(z_s, g_q_lat[l], cos_s, sin_s, t_s, qa_col, wq_b)
    q_lat_s, q_rope_hs = _absorb(q_full_s, wukT_b)
    rows = seq_s * n_heads
    q_rope_s = jnp.transpose(q_rope_hs, (1, 0, 2)).reshape(n_seq_s, rows, QK_ROPE)
    o_lat_s = _paged_attn(page_table, q_lat_s.reshape(n_seq_s, rows, kv_lora), q_rope_s,
                          kv_lat_s.reshape(n_seq_s, seq_s, kv_lora), k_rope_s.reshape(n_seq_s, seq_s, QK_ROPE),
                          cache_kv_latent, jnp.swapaxes(cache_k_rope, 2, 3), n_heads)
    attn_s = _v_up(o_lat_s.reshape(t_s, n_heads * kv_lora), wuv_h_b)
    hist = state_conv.shape[2]
    state_s = state_conv.reshape(n_seq_s, hist, conv_ch)
    conv_s_t, u_s = _conv_sample(state_s, z_s.reshape(n_seq_s, seq_s, -1), conv_ch, 0,
                                 conv_w[l], conv_b[l], conv_ln_g[l], conv_ln_b[l])
    conv_s = jnp.transpose(conv_s_t, (1, 0, 2)).reshape(t_s, conv_ch)
    x1_s, hp_s, lg_s = _outproj(attn_s, conv_s, w_out_b, xs, mod_s, g_post_mix[l], g_pre_ffn[l], wr_b, br)

    hp_all = jnp.concatenate([hp_p, hp_s], axis=0)
    ids, wts = _router(jnp.concatenate([lg_p, lg_s], axis=0))
    tm_e = 512
    nt_max = (TOP_K * t_all) // tm_e + N_EXPERTS
    tile_expert, tile_blk, tile_first, n_active, src, pos = _routing_tables(ids[:, :TOP_K], t_all, tm_e, nt_max)
    xs_sorted = _token_gather(src, tile_blk, n_active, hp_all, tm_e, d)
    h_act = _moe_up(tile_expert, tile_blk, tile_first, n_active, xs_sorted, w_exp_gate, w_exp_up, tm_e)
    y_slots = _moe_down(tile_expert, tile_blk, tile_first, n_active, h_act, w_exp_down, tm_e)

    y_p = _final(x1_p, y_slots, pos, wts, mod_p, g_post_ffn[l], 0, t_all)
    y_s = _final(x1_s, y_slots, pos, wts, mod_s, g_post_ffn[l], t_p, t_all)

    conv_state_p = jnp.transpose(st_p, (0, 2, 1, 3)).reshape(n_seq_p, HALO, conv_ch)[:, HALO - (CONV_WIDTH - 1):, :]
    conv_state_s = jnp.concatenate([state_s[:, seq_s:, :], u_s], axis=1)
    return (y_p.reshape(n_seq_p, seq_p, d), y_s.reshape(n_seq_s, seq_s, d),
            kv_lat_p.reshape(1, n_seq_p, seq_p, kv_lora), k_rope_p.reshape(1, n_seq_p, seq_p, QK_ROPE),
            conv_state_p[None], kv_lat_s.reshape(1, n_seq_s, seq_s, kv_lora),
            k_rope_s.reshape(1, n_seq_s, seq_s, QK_ROPE), conv_state_s[None])
```
